```python
import math
import jax, jax.numpy as jnp
from jax import lax
import numpy as np

D_MODEL = 1024
BATCH = 2
SEQ = 16384
DEPTH = 4
DEC_BATCH = 16
DEC_SEQ = 4096
PAST_LEN = 128

POOL_GROUPS = 4
POOL_GROUP_DIM = 64
POOL_WIDTH = POOL_GROUPS * POOL_GROUP_DIM
POOL_WINDOWS = (2, 4, 8, 16)

HGRN_HEADS = 4
HGRN_HEAD_DIM = 64
HGRN_WIDTH = HGRN_HEADS * HGRN_HEAD_DIM
HGRN_CHUNK = 64

DIFF_HEADS = 4
DIFF_HEAD_DIM = 64
DIFF_V_DIM = 2 * DIFF_HEAD_DIM
DIFF_QK_WIDTH = DIFF_HEADS * 2 * DIFF_HEAD_DIM
DIFF_WIDTH = DIFF_HEADS * DIFF_V_DIM
ATTN_BLOCK = 128

D_MIX = POOL_WIDTH + HGRN_WIDTH + DIFF_WIDTH
IN_WIDTHS = (POOL_WIDTH,) + (HGRN_WIDTH,) * 5 + (DIFF_QK_WIDTH, DIFF_QK_WIDTH, DIFF_WIDTH)
IN_WIDTH = sum(IN_WIDTHS)
SPLIT_POINTS = tuple(sum(IN_WIDTHS[:i + 1]) for i in range(len(IN_WIDTHS) - 1))

D_FF = 2816
N_EXPERTS = 8
TOP_K = 2
D_FF_EXPERT = 2816
N_DENSE = (DEPTH + 1) // 2
N_MOE = DEPTH // 2

RMS_EPS = 1e-6

kernel_name = 'hybrid_pool_hgrn2_diffattn_encoder'


def _rmsnorm(x, g):
    xf = x.astype(jnp.float32)
    y = xf * lax.rsqrt(jnp.mean(xf * xf, axis=-1, keepdims=True) + RMS_EPS)
    return (y * g.astype(jnp.float32)).astype(x.dtype)


def _alibi_slopes(n):
    return jnp.asarray([2.0 ** (-8.0 * (h + 1) / n) for h in range(n)], dtype=jnp.float32)


def _pool_mixer(u, w_pool, scale):
    B, S, _ = u.shape
    uf = u.astype(jnp.float32)
    cs = jnp.concatenate([jnp.zeros((B, 1, POOL_WIDTH), jnp.float32), jnp.cumsum(uf, axis=1)], axis=1)
    t = jnp.arange(S)
    outs = []
    for gi, w in enumerate(POOL_WINDOWS):
        lo = jnp.clip(t - w // 2, 0, S)
        hi = jnp.clip(t + w // 2, 0, S)
        sl = slice(gi * POOL_GROUP_DIM, (gi + 1) * POOL_GROUP_DIM)
        csg = cs[:, :, sl]
        mean = (jnp.take(csg, hi, axis=1) - jnp.take(csg, lo, axis=1)) / (hi - lo).astype(jnp.float32)[None, :, None]
        outs.append(mean - uf[:, :, sl])
    d = jnp.stack(outs, axis=2).astype(u.dtype)
    y = jnp.einsum('bsgc,gcd->bsgd', d, w_pool).reshape(B, S, POOL_WIDTH)
    return y * scale


def _gla_chunk_scan(q, k, v, g):
    B, S, H, dk = q.shape
    dv = v.shape[-1]
    C = HGRN_CHUNK
    N = S // C

    def to_chunks(a):
        return a.reshape(B, N, C, H, a.shape[-1]).transpose(1, 0, 3, 2, 4)

    mask = jnp.tril(jnp.ones((C, C), dtype=bool))[None, None, :, :, None]

    def step(st, xs):
        qc, kc, vc, gc = xs
        b = jnp.cumsum(gc, axis=2)
        b_last = b[:, :, -1:, :]
        o_inter = jnp.einsum('bhtd,bhde->bhte', qc * jnp.exp(b), st)
        rel = jnp.where(mask, b[:, :, :, None, :] - b[:, :, None, :, :], 0.0)
        decay = jnp.where(mask, jnp.exp(rel), 0.0)
        A = jnp.einsum('bhtsd,bhsd->bhts', qc[:, :, :, None, :] * decay, kc)
        o = o_inter + jnp.einsum('bhts,bhse->bhte', A, vc)
        st_new = jnp.exp(b_last[:, :, 0, :])[..., None] * st + jnp.einsum('bhsd,bhse->bhde', kc * jnp.exp(b_last - b), vc)
        return st_new, o

    st0 = jnp.zeros((B, H, dk, dv), jnp.float32)
    _, o = lax.scan(step, st0, (to_chunks(q), to_chunks(k), to_chunks(v), to_chunks(g)))
    return o.transpose(1, 0, 3, 2, 4).reshape(B, S, H, dv)


def _hgrn2_mixer(q, fz_f, fz_b, i, gate, lb_f, lb_b, norm_g):
    B, S, _ = q.shape

    def heads(a):
        return a.astype(jnp.float32).reshape(B, S, HGRN_HEADS, HGRN_HEAD_DIM)

    qh = jax.nn.silu(heads(q))
    vh = heads(i)

    def decay(z, lb):
        z = heads(z)
        lb = lb.reshape(HGRN_HEADS, HGRN_HEAD_DIM)
        f = lb + (1.0 - lb) * jax.nn.sigmoid(z)
        logf = jnp.log(jnp.maximum(f, 1e-30))
        kk = (1.0 - lb) * jax.nn.sigmoid(-z)
        return logf, kk

    g_f, k_f = decay(fz_f, lb_f)
    g_b, k_b = decay(fz_b, lb_b)
    flip = lambda a: a[:, ::-1]
    o_f = _gla_chunk_scan(qh, k_f, vh, g_f)
    o_b = flip(_gla_chunk_scan(flip(qh), flip(k_b), flip(vh), flip(g_b)))
    o = o_f + o_b
    o = o * lax.rsqrt(jnp.mean(o * o, axis=-1, keepdims=True) + RMS_EPS) * norm_g.astype(jnp.float32)
    o = o.reshape(B, S, HGRN_WIDTH) * jax.nn.silu(gate.astype(jnp.float32))
    return o.astype(q.dtype)


def _diff_attention(q, k, v, lam, lam_init, norm_g):
    B, S, _ = q.shape
    H, d = DIFF_HEADS, DIFF_HEAD_DIM
    qh = q.reshape(B, S, H, 2, d).transpose(0, 2, 3, 1, 4) * (d ** -0.5)
    kh = k.reshape(B, S, H, 2, d).transpose(0, 2, 3, 1, 4)
    vf = v.reshape(B, S, H, DIFF_V_DIM).transpose(0, 2, 1, 3).astype(jnp.float32)
    nb = S // ATTN_BLOCK
    q_blocks = qh.reshape(B, H, 2, nb, ATTN_BLOCK, d).transpose(3, 0, 1, 2, 4, 5)
    slopes = _alibi_slopes(H)
    tk = jnp.arange(S)

    def block(xs):
        qb, start = xs
        tq = start + jnp.arange(ATTN_BLOCK)
        dist = jnp.abs(tq[:, None] - tk[None, :]).astype(jnp.float32)
        bias = -slopes[:, None, None] * dist
        s = jnp.einsum('bhmqd,bhmkd->bhmqk', qb, kh).astype(jnp.float32) + bias[None, :, None]
        p = jax.nn.softmax(s, axis=-1)
        a = p[:, :, 0] - lam * p[:, :, 1]
        return jnp.einsum('bhqk,bhkd->bhqd', a, vf)

    starts = jnp.arange(nb, dtype=jnp.int32) * ATTN_BLOCK
    o = lax.map(block, (q_blocks, starts))
    o = o.transpose(1, 0, 3, 2, 4).reshape(B, S, H, DIFF_V_DIM)
    o = o * lax.rsqrt(jnp.mean(o * o, axis=-1, keepdims=True) + RMS_EPS) * norm_g.astype(jnp.float32)
    o = o * (1.0 - lam_init)
    return o.reshape(B, S, DIFF_WIDTH).astype(q.dtype)


def _swiglu(x, w_gate, w_up, w_down):
    return (jax.nn.silu(x @ w_gate) * (x @ w_up)) @ w_down


def _moe_ffn(h, router_w, router_b, w_gate, w_up, w_down):
    B, S, D = h.shape
    t = h.reshape(B * S, D)
    logits = (t @ router_w).astype(jnp.float32) + router_b.astype(jnp.float32)
    top_v, top_i = lax.top_k(logits, TOP_K)
    probs = jax.nn.softmax(top_v, axis=-1)
    combine = jnp.sum(jax.nn.one_hot(top_i, N_EXPERTS, dtype=jnp.float32) * probs[..., None], axis=1)
    y = jnp.zeros((B * S, D), jnp.float32)
    for e in range(N_EXPERTS):
        y = y + combine[:, e:e + 1] * _swiglu(t, w_gate[e], w_up[e], w_down[e]).astype(jnp.float32)
    return y.reshape(B, S, D).astype(h.dtype)


def _trunk(x, c, ada_w, ada_b, norm1_g, norm2_g, w_in, pool_w, pool_scale, hgrn_lb, hgrn_norm_g,
           diff_lambda, diff_norm_g, w_out, ffn_w_gate, ffn_w_up, ffn_w_down, router_w, router_b,
           moe_w_gate, moe_w_up, moe_w_down, final_norm_g):
    p_lb = jax.nn.softmax(hgrn_lb.astype(jnp.float32), axis=0)
    lbs = jnp.cumsum(p_lb, axis=0) - p_lb[0:1]
    c_act = jax.nn.silu(c)
    for l in range(DEPTH):
        ada = (c_act @ ada_w[l] + ada_b[l])[:, None, :]
        sh1, sc1, g1, sh2, sc2, g2 = jnp.split(ada, 6, axis=-1)
        h = _rmsnorm(x, norm1_g[l]) * (1.0 + sc1) + sh1
        proj = h @ w_in[l]
        u, hq, hf_f, hf_b, hi, hg, dq, dk, dv = jnp.split(proj, SPLIT_POINTS, axis=-1)
        pool_out = _pool_mixer(u, pool_w[l], pool_scale[l])
        hgrn_out = _hgrn2_mixer(hq, hf_f, hf_b, hi, hg, lbs[l, 0], lbs[l, 1], hgrn_norm_g[l])
        lam_init = 0.8 - 0.6 * math.exp(-0.3 * l)
        lv = diff_lambda[l].astype(jnp.float32)
        lam = jnp.exp(jnp.sum(lv[0] * lv[1])) - jnp.exp(jnp.sum(lv[2] * lv[3])) + lam_init
        diff_out = _diff_attention(dq, dk, dv, lam, lam_init, diff_norm_g[l])
        mixed = jnp.concatenate([pool_out.astype(h.dtype), hgrn_out.astype(h.dtype), diff_out.astype(h.dtype)], axis=-1)
        x = x + g1 * (mixed @ w_out[l])
        h2 = _rmsnorm(x, norm2_g[l]) * (1.0 + sc2) + sh2
        if l % 2 == 0:
            j = l // 2
            f = _swiglu(h2, ffn_w_gate[j], ffn_w_up[j], ffn_w_down[j])
        else:
            j = l // 2
            f = _moe_ffn(h2, router_w[j], router_b[j], moe_w_gate[j], moe_w_up[j], moe_w_down[j])
        x = x + g2 * f
    return _rmsnorm(x, final_norm_g)


def setup_inputs(seed: int = 0) -> dict:
    key = jax.random.key(seed)
    ks = jax.random.split(key, 32)
    D = D_MODEL

    def nrm(k, shape, s):
        return jax.random.normal(k, shape, jnp.float32) * s

    return {
        'x_prompt': nrm(ks[0], (BATCH, SEQ, D), 1.0),
        'x_sample': nrm(ks[1], (DEC_BATCH, DEC_SEQ, D), 1.0),
        'c_prompt': nrm(ks[2], (BATCH, D), 1.0),
        'c_sample': nrm(ks[3], (DEC_BATCH, D), 1.0),
        'ada_w': nrm(ks[4], (DEPTH, D, 6 * D), 0.5 * D ** -0.5),
        'ada_b': nrm(ks[5], (DEPTH, 6 * D), 0.01),
        'norm1_g': 1.0 + nrm(ks[6], (DEPTH, D), 0.05),
        'norm2_g': 1.0 + nrm(ks[7], (DEPTH, D), 0.05),
        'w_in': nrm(ks[8], (DEPTH, D, IN_WIDTH), D ** -0.5),
        'pool_w': nrm(ks[9], (DEPTH, POOL_GROUPS, POOL_GROUP_DIM, POOL_GROUP_DIM), POOL_GROUP_DIM ** -0.5),
        'pool_scale': 1.0 + nrm(ks[10], (DEPTH, POOL_WIDTH), 0.1),
        'hgrn_lb': nrm(ks[11], (DEPTH, 2, HGRN_WIDTH), 0.5),
        'hgrn_norm_g': 1.0 + nrm(ks[12], (DEPTH, HGRN_HEAD_DIM), 0.05),
        'diff_lambda': nrm(ks[13], (DEPTH, 4, DIFF_HEAD_DIM), 0.1),
        'diff_norm_g': 1.0 + nrm(ks[14], (DEPTH, DIFF_V_DIM), 0.05),
        'w_out': nrm(ks[15], (DEPTH, D_MIX, D), D_MIX ** -0.5),
        'ffn_w_gate': nrm(ks[16], (N_DENSE, D, D_FF), D ** -0.5),
        'ffn_w_up': nrm(ks[17], (N_DENSE, D, D_FF), D ** -0.5),
        'ffn_w_down': nrm(ks[18], (N_DENSE, D_FF, D), D_FF ** -0.5),
        'router_w': nrm(ks[19], (N_MOE, D, N_EXPERTS), D ** -0.5),
        'router_b': nrm(ks[20], (N_MOE, N_EXPERTS), 0.01),
        'moe_w_gate': nrm(ks[21], (N_MOE, N_EXPERTS, D, D_FF_EXPERT), D ** -0.5),
        'moe_w_up': nrm(ks[22], (N_MOE, N_EXPERTS, D, D_FF_EXPERT), D ** -0.5),
        'moe_w_down': nrm(ks[23], (N_MOE, N_EXPERTS, D_FF_EXPERT, D), D_FF_EXPERT ** -0.5),
        'final_norm_g': 1.0 + nrm(ks[24], (D,), 0.05),
    }


def reference(x_prompt, x_sample, c_prompt, c_sample, ada_w, ada_b, norm1_g, norm2_g, w_in, pool_w,
              pool_scale, hgrn_lb, hgrn_norm_g, diff_lambda, diff_norm_g, w_out, ffn_w_gate, ffn_w_up,
              ffn_w_down, router_w, router_b, moe_w_gate, moe_w_up, moe_w_down, final_norm_g):
    y_prompt = _trunk(x_prompt, c_prompt, ada_w, ada_b, norm1_g, norm2_g, w_in, pool_w, pool_scale,
                      hgrn_lb, hgrn_norm_g, diff_lambda, diff_norm_g, w_out, ffn_w_gate, ffn_w_up,
                      ffn_w_down, router_w, router_b, moe_w_gate, moe_w_up, moe_w_down, final_norm_g)
    y_sample = _trunk(x_sample, c_sample, ada_w, ada_b, norm1_g, norm2_g, w_in, pool_w, pool_scale,
                      hgrn_lb, hgrn_norm_g, diff_lambda, diff_norm_g, w_out, ffn_w_gate, ffn_w_up,
                      ffn_w_down, router_w, router_b, moe_w_gate, moe_w_up, moe_w_down, final_norm_g)
    return (y_prompt, y_sample)
```

```python
import functools
import math

import jax
import jax.numpy as jnp
from jax import lax
from jax.experimental import pallas as pl
from jax.experimental.pallas import tpu as pltpu

D_MODEL = 1024
DEPTH = 4
POOL_GROUP_DIM = 64
POOL_WIDTH = 256
POOL_WINDOWS = (2, 4, 8, 16)
POOL_HALO = 16
HGRN_HEAD_DIM = 64
HGRN_WIDTH = 256
DIFF_HEADS = 4
DIFF_HEAD_DIM = 64
DIFF_V_DIM = 128
DIFF_QK_WIDTH = 512
DIFF_WIDTH = 512
D_FF = 2816
N_EXPERTS = 8
RMS_EPS = 1e-6

LANES = 128
VMEM_LIMIT_BYTES = 56 * 1024 * 1024

F32 = jnp.float32
BF16 = jnp.bfloat16


def _params(*semantics):
    return pltpu.CompilerParams(dimension_semantics=semantics, vmem_limit_bytes=VMEM_LIMIT_BYTES)


def _split2(a):
    hi = a.astype(BF16)
    lo = (a - hi.astype(F32)).astype(BF16)
    return hi, lo


def _split3(a):
    hi = a.astype(BF16)
    r = a - hi.astype(F32)
    mid = r.astype(BF16)
    lo = (r - mid.astype(F32)).astype(BF16)
    return hi, mid, lo


def _dot(a, b):
    return jnp.dot(a, b, preferred_element_type=F32)


def _dot_nt(a, b):
    return lax.dot_general(a, b, (((1,), (1,)), ((), ())), preferred_element_type=F32)


def _dot_tn(a, b):
    return lax.dot_general(a, b, (((0,), (0,)), ((), ())), preferred_element_type=F32)


def _dot_f32(a, b):
    ah, al = _split2(a)
    bh, bl = _split2(b)
    return _dot(ah, bh) + _dot(ah, bl) + _dot(al, bh)


def _silu(a):
    return a * jax.nn.sigmoid(a)


def _rms_modulate(x, norm_g, scale, shift):
    y = x * lax.rsqrt(jnp.mean(x * x, axis=-1, keepdims=True) + RMS_EPS) * norm_g
    return y * (1.0 + scale) + shift


def _ada_kernel(c_ref, w_ref, b_ref, o_ref):
    o_ref[0] = _dot_f32(_silu(c_ref[...]), w_ref[0]) + b_ref[0]


def _ada(c_all, ada_w, ada_b):
    rows, d = c_all.shape
    tn = 1024
    n_out = ada_w.shape[-1]
    return pl.pallas_call(
        _ada_kernel,
        grid=(DEPTH, n_out // tn),
        in_specs=[
            pl.BlockSpec((rows, d), lambda l, n: (0, 0)),
            pl.BlockSpec((1, d, tn), lambda l, n: (l, 0, n)),
            pl.BlockSpec((1, 1, tn), lambda l, n: (l, 0, n)),
        ],
        out_specs=pl.BlockSpec((1, rows, tn), lambda l, n: (l, 0, n)),
        out_shape=jax.ShapeDtypeStruct((DEPTH, rows, n_out), F32),
        compiler_params=_params("arbitrary", "arbitrary"),
        name="ada",
    )(c_all, ada_w, ada_b.reshape(DEPTH, 1, n_out))


_PROJ_SPLITS = (
    (0, POOL_WIDTH),
    (POOL_WIDTH, POOL_WIDTH + 5 * HGRN_WIDTH),
    (1536, 2048),
    (2048, 2560),
    (2560, 3072),
)


def _norm_proj_kernel(*refs, has_f):
    if has_f:
        x_ref, f_ref, g2_ref, ng_ref, sc_ref, sh_ref, w_ref, xo_ref = refs[:8]
        outs = refs[8:]
        x = x_ref[0] + g2_ref[0] * f_ref[0]
        xo_ref[0] = x
    else:
        x_ref, ng_ref, sc_ref, sh_ref, w_ref = refs[:5]
        outs = refs[5:]
        x = x_ref[0]
    hb = _rms_modulate(x, ng_ref[...], sc_ref[0], sh_ref[0]).astype(BF16)
    for o_ref, (a, b) in zip(outs, _PROJ_SPLITS):
        o_ref[0] = _dot(hb, w_ref[:, a:b]).astype(BF16)


def _norm_proj(x, f, g2, norm_g, sc, sh, w_in, tm=512):
    B, S, D = x.shape
    has_f = f is not None
    row = pl.BlockSpec((1, tm, D), lambda b, i: (b, i, 0))
    per_b = pl.BlockSpec((1, 1, D), lambda b, i: (b, 0, 0))
    in_specs = [row]
    args = [x]
    if has_f:
        in_specs += [row, per_b]
        args += [f, g2]
    in_specs += [pl.BlockSpec((1, D), lambda b, i: (0, 0)), per_b, per_b,
                 pl.BlockSpec(w_in.shape, lambda b, i: (0, 0))]
    args += [norm_g.reshape(1, D), sc, sh, w_in]
    out_specs, out_shape = [], []
    if has_f:
        out_specs.append(row)
        out_shape.append(jax.ShapeDtypeStruct((B, S, D), F32))
    for a, b_ in _PROJ_SPLITS:
        out_specs.append(pl.BlockSpec((1, tm, b_ - a), lambda b, i: (b, i, 0)))
        out_shape.append(jax.ShapeDtypeStruct((B, S, b_ - a), BF16))
    res = pl.pallas_call(
        functools.partial(_norm_proj_kernel, has_f=has_f),
        grid=(B, S // tm),
        in_specs=in_specs, out_specs=out_specs, out_shape=out_shape,
        compiler_params=_params("parallel", "parallel"),
        name="norm_proj",
    )(*args)
    if has_f:
        return res[0], res[1:]
    return x, res


def _pool_kernel(prev_ref, u_ref, next_ref, w_ref, scale_ref, o_ref, *, seq_len):
    i = pl.program_id(1)
    n = pl.num_programs(1)
    tm = u_ref.shape[1]
    u = u_ref[0].astype(F32)
    prev = jnp.where(i > 0, prev_ref[0].astype(F32), 0.0)
    nxt = jnp.where(i < n - 1, next_ref[0].astype(F32), 0.0)
    ext = jnp.concatenate([prev, u, nxt], axis=0)
    rows = ext.shape[0]
    group = lax.broadcasted_iota(jnp.int32, (tm, POOL_WIDTH), 1) // POOL_GROUP_DIM
    s = ext
    win = jnp.zeros((tm, POOL_WIDTH), F32)
    for gi, w in enumerate(POOL_WINDOWS):
        s = s + pltpu.roll(s, w // 2, axis=0)
        lead = w // 2 - 1
        centred = s if lead == 0 else pltpu.roll(s, rows - lead, axis=0)
        win = jnp.where(group == gi, centred[POOL_HALO:POOL_HALO + tm], win)
    t = i * tm + lax.broadcasted_iota(jnp.int32, (tm, POOL_WIDTH), 0)
    half = jnp.left_shift(1, group)
    count = jnp.minimum(t + half, seq_len) - jnp.maximum(t - half, 0)
    d = win / count.astype(F32) - u
    o_ref[0] = (_dot(d.astype(BF16), w_ref[...]) * scale_ref[...]).astype(BF16)


def _pool(u, w_bd, scale, tm=512):
    B, S, W = u.shape
    hb = tm // POOL_HALO
    nh = S // POOL_HALO
    return pl.pallas_call(
        functools.partial(_pool_kernel, seq_len=S),
        grid=(B, S // tm),
        in_specs=[
            pl.BlockSpec((1, POOL_HALO, W), lambda b, i: (b, jnp.maximum(i * hb - 1, 0), 0)),
            pl.BlockSpec((1, tm, W), lambda b, i: (b, i, 0)),
            pl.BlockSpec((1, POOL_HALO, W), lambda b, i: (b, jnp.minimum((i + 1) * hb, nh - 1), 0)),
            pl.BlockSpec((W, W), lambda b, i: (0, 0)),
            pl.BlockSpec((1, W), lambda b, i: (0, 0)),
        ],
        out_specs=pl.BlockSpec((1, tm, W), lambda b, i: (b, i, 0)),
        out_shape=jax.ShapeDtypeStruct((B, S, W), BF16),
        compiler_params=_params("parallel", "parallel"),
        name="pool",
    )(u, u, u, w_bd, scale.reshape(1, W))


HGRN_SUB = 16
HGRN_GROUP = 128


def _hgrn_kernel(*refs, rev, final):
    if final:
        (q_ref, z_ref, v_ref, lb_ref, ones_ref, mask_ref, of_ref, gate_ref, ng_ref,
         o_ref, st_ref, qe_sc, kt_sc, vb_sc, gam_sc, acc_sc) = refs
    else:
        (q_ref, z_ref, v_ref, lb_ref, ones_ref, mask_ref,
         o_ref, st_ref, qe_sc, kt_sc, vb_sc, gam_sc, acc_sc) = refs
    c = HGRN_SUB
    tt = q_ref.shape[1]
    width = q_ref.shape[2]

    @pl.when(pl.program_id(1) == 0)
    def _():
        st_ref[...] = jnp.zeros_like(st_ref)

    q = _silu(q_ref[0].astype(F32))
    z = z_ref[0].astype(F32)
    v = v_ref[0].astype(F32)
    lb = lb_ref[...]
    f = jnp.maximum(lb + (1.0 - lb) * jax.nn.sigmoid(z), 1e-30)
    g = jnp.log(f)
    kk = (1.0 - lb) * jax.nn.sigmoid(-z)

    r_i = lax.broadcasted_iota(jnp.int32, (HGRN_GROUP, HGRN_GROUP), 0)
    s_i = lax.broadcasted_iota(jnp.int32, (HGRN_GROUP, HGRN_GROUP), 1)
    same = (r_i // c) == (s_i // c)
    tri = (same & ((s_i >= r_i) if rev else (s_i <= r_i))).astype(BF16)
    blk = same.astype(BF16)
    cums, tots = [], []
    for g0 in range(0, tt, HGRN_GROUP):
        parts = _split3(g[g0:g0 + HGRN_GROUP])
        cums.append(sum(_dot(tri, p) for p in parts))
        tots.append(sum(_dot(blk, p) for p in parts))
    cum = jnp.concatenate(cums, axis=0)
    tot = jnp.concatenate(tots, axis=0)

    qe_sc[...] = (q * jnp.exp(cum)).astype(BF16)
    kt_sc[...] = (kk * jnp.exp(tot - cum)).astype(BF16)
    vb_sc[...] = v.astype(BF16)
    gam_sc[...] = jnp.exp(tot)

    pos = lax.broadcasted_iota(jnp.int32, (tt, width), 0) % c
    ones_bd = ones_ref[...]
    step = tt - 1 if rev else 1
    k_sh, v_sh, f_sh = kk, v, f
    decay = jnp.ones_like(f)
    acc = _dot((q * kk).astype(BF16), ones_bd) * v
    for delta in range(1, c):
        k_sh = pltpu.roll(k_sh, step, axis=0)
        v_sh = pltpu.roll(v_sh, step, axis=0)
        decay = decay * f_sh
        f_sh = pltpu.roll(f_sh, step, axis=0)
        valid = (pos + delta <= c - 1) if rev else (pos >= delta)
        p = jnp.where(valid, q * k_sh * decay, 0.0)
        acc = acc + _dot(p.astype(BF16), ones_bd) * v_sh
    acc_sc[...] = acc

    n_sub = tt // c
    mask = mask_ref[...]

    def body(ci, carry):
        idx = (n_sub - 1 - ci) if rev else ci
        r0 = pl.multiple_of(idx * c, c)
        st = st_ref[...]
        acc_sc[pl.ds(r0, c), :] += _dot_nt(qe_sc[pl.ds(r0, c), :], st.astype(BF16))
        kv = _dot_tn(vb_sc[pl.ds(r0, c), :], kt_sc[pl.ds(r0, c), :])
        st_ref[...] = st * gam_sc[pl.ds(r0, 8), :][0:1] + kv * mask
        return carry

    lax.fori_loop(0, n_sub, body, 0)

    o = acc_sc[...]
    if final:
        o = o + of_ref[0]
        ms = sum(_dot(p, ones_bd) for p in _split2(o * o)) * (1.0 / HGRN_HEAD_DIM)
        o = o * lax.rsqrt(ms + RMS_EPS) * ng_ref[...] * _silu(gate_ref[0].astype(F32))
        o_ref[0] = o.astype(BF16)
    else:
        o_ref[0] = o


def _hgrn_direction(hg, lb, ones_bd, mask_bd, rev, o_fwd=None, norm_g=None, tt=256):
    B, S, _ = hg.shape
    W = HGRN_WIDTH
    nt = S // tt
    final = o_fwd is not None
    tile = (lambda j: nt - 1 - j) if rev else (lambda j: j)

    def col(k):
        return pl.BlockSpec((1, tt, W), lambda b, j: (b, tile(j), k))

    const = lambda shape: pl.BlockSpec(shape, lambda b, j: (0, 0))
    in_specs = [col(0), col(2 if rev else 1), col(3), const((1, W)), const((W, W)), const((W, W))]
    args = [hg, hg, hg, lb.reshape(1, W), ones_bd, mask_bd]
    if final:
        in_specs += [pl.BlockSpec((1, tt, W), lambda b, j: (b, tile(j), 0)), col(4), const((1, W))]
        args += [o_fwd, hg, norm_g.reshape(1, W)]
    return pl.pallas_call(
        functools.partial(_hgrn_kernel, rev=rev, final=final),
        grid=(B, nt),
        in_specs=in_specs,
        out_specs=pl.BlockSpec((1, tt, W), lambda b, j: (b, tile(j), 0)),
        out_shape=jax.ShapeDtypeStruct((B, S, W), BF16 if final else F32),
        scratch_shapes=[
            pltpu.VMEM((W, W), F32),
            pltpu.VMEM((tt, W), BF16),
            pltpu.VMEM((tt, W), BF16),
            pltpu.VMEM((tt, W), BF16),
            pltpu.VMEM((tt, W), F32),
            pltpu.VMEM((tt, W), F32),
        ],
        compiler_params=_params("parallel", "arbitrary"),
        name="hgrn_bwd" if rev else "hgrn_fwd",
    )(*args)


def _diff_attn_kernel(scal_ref, q_ref, k_ref, v_ref, ng_ref, o_ref, qx_sc, d0_sc, m_sc, l_sc, acc_sc):
    h = pl.program_id(1)
    qi = pl.program_id(2)
    ki = pl.program_id(3)
    tq = q_ref.shape[1]
    tk = k_ref.shape[1]

    @pl.when(ki == 0)
    def _():
        q = q_ref[0].astype(F32) * (DIFF_HEAD_DIM ** -0.5)
        lane = lax.broadcasted_iota(jnp.int32, q.shape, 1)
        first = lane < DIFF_HEAD_DIM
        qx_sc[0:tq, :] = jnp.where(first, q, 0.0).astype(BF16)
        qx_sc[tq:2 * tq, :] = jnp.where(first, 0.0, q).astype(BF16)
        r = lax.broadcasted_iota(jnp.int32, (2 * tq, tk), 0) % tq
        cidx = lax.broadcasted_iota(jnp.int32, (2 * tq, tk), 1)
        d0_sc[...] = (r - cidx).astype(F32)
        m_sc[...] = jnp.full(m_sc.shape, -jnp.inf, F32)
        l_sc[...] = jnp.zeros(l_sc.shape, F32)
        acc_sc[...] = jnp.zeros(acc_sc.shape, F32)

    slope = scal_ref[2 + h]
    offset = (qi * tq - ki * tk).astype(F32)
    s = _dot_nt(qx_sc[...], k_ref[0]) - slope * jnp.abs(d0_sc[...] + offset)
    m_old = m_sc[...]
    m_new = jnp.maximum(m_old, jnp.max(s, axis=-1, keepdims=True))
    alpha = jnp.exp(m_old - m_new)
    p = jnp.exp(s - m_new[:, 0:1])
    l_sc[...] = alpha * l_sc[...] + jnp.sum(p, axis=-1, keepdims=True)
    acc_sc[...] = alpha * acc_sc[...] + _dot(p.astype(BF16), v_ref[0])
    m_sc[...] = m_new

    @pl.when(ki == pl.num_programs(3) - 1)
    def _():
        lam = scal_ref[0]
        out_scale = scal_ref[1]
        o_all = acc_sc[...] / l_sc[...]
        o = o_all[0:tq] - lam * o_all[tq:2 * tq]
        o = o * lax.rsqrt(jnp.mean(o * o, axis=-1, keepdims=True) + RMS_EPS)
        o_ref[0] = (o * ng_ref[...] * out_scale).astype(BF16)


def _diff_attn(dq, dk, dv, scal, norm_g, tq=512, tk=512):
    B, S, _ = dq.shape
    grid_spec = pltpu.PrefetchScalarGridSpec(
        num_scalar_prefetch=0,
        grid=(B, DIFF_HEADS, S // tq, S // tk),
        in_specs=[
            pl.BlockSpec(memory_space=pltpu.SMEM),
            pl.BlockSpec((1, tq, LANES), lambda b, h, i, j: (b, i, h)),
            pl.BlockSpec((1, tk, LANES), lambda b, h, i, j: (b, j, h)),
            pl.BlockSpec((1, tk, LANES), lambda b, h, i, j: (b, j, h)),
            pl.BlockSpec((1, DIFF_V_DIM), lambda b, h, i, j: (0, 0)),
        ],
        out_specs=pl.BlockSpec((1, tq, DIFF_V_DIM), lambda b, h, i, j: (b, i, h)),
        scratch_shapes=[
            pltpu.VMEM((2 * tq, LANES), BF16),
            pltpu.VMEM((2 * tq, tk), F32),
            pltpu.VMEM((2 * tq, LANES), F32),
            pltpu.VMEM((2 * tq, LANES), F32),
            pltpu.VMEM((2 * tq, DIFF_V_DIM), F32),
        ],
    )
    return pl.pallas_call(
        _diff_attn_kernel,
        grid_spec=grid_spec,
        out_shape=jax.ShapeDtypeStruct((B, S, DIFF_WIDTH), BF16),
        compiler_params=_params("parallel", "parallel", "parallel", "arbitrary"),
        name="diff_attn",
    )(scal, dq, dk, dv, norm_g.reshape(1, DIFF_V_DIM))


def _out_proj_kernel(*refs, has_router):
    (x_ref, pool_ref, hgrn_ref, diff_ref, w_ref, g1_ref, ng_ref, sc_ref, sh_ref) = refs[:9]
    if has_router:
        rw_ref, rb_ref, xo_ref, h2_ref, comb_ref = refs[9:]
    else:
        xo_ref, h2_ref = refs[9:]
    a, b = POOL_WIDTH, POOL_WIDTH + HGRN_WIDTH
    mixed = (_dot(pool_ref[0], w_ref[0:a, :]) + _dot(hgrn_ref[0], w_ref[a:b, :])
             + _dot(diff_ref[0], w_ref[b:, :]))
    x = x_ref[0] + g1_ref[0] * mixed
    xo_ref[0] = x
    h2 = _rms_modulate(x, ng_ref[...], sc_ref[0], sh_ref[0])
    h2_ref[0] = h2.astype(BF16)
    if has_router:
        hh, hl = _split2(h2)
        wh, wl = _split2(rw_ref[...])
        logits = _dot_nt(wh, hh) + _dot_nt(wh, hl) + _dot_nt(wl, hh) + rb_ref[...]
        e_idx = lax.broadcasted_iota(jnp.int32, logits.shape, 0)
        m1 = jnp.max(logits, axis=0, keepdims=True)
        i1 = jnp.min(jnp.where(logits == m1, e_idx, N_EXPERTS), axis=0, keepdims=True)
        first = e_idx == i1
        rest = jnp.where(first, -jnp.inf, logits)
        m2 = jnp.max(rest, axis=0, keepdims=True)
        i2 = jnp.min(jnp.where(rest == m2, e_idx, N_EXPERTS), axis=0, keepdims=True)
        second = e_idx == i2
        e2 = jnp.exp(m2 - m1)
        p1 = 1.0 / (1.0 + e2)
        comb_ref[...] = jnp.where(first, p1, jnp.where(second, e2 * p1, 0.0))


def _out_proj(x, pool_o, hgrn_o, diff_o, w_out, g1, norm_g, sc, sh, router=None, tm=512):
    B, S, D = x.shape
    nt = S // tm
    has_router = router is not None
    row = lambda w: pl.BlockSpec((1, tm, w), lambda b, i: (b, i, 0))
    per_b = pl.BlockSpec((1, 1, D), lambda b, i: (b, 0, 0))
    const = lambda shape: pl.BlockSpec(shape, lambda b, i: (0,) * len(shape))
    in_specs = [row(D), row(POOL_WIDTH), row(HGRN_WIDTH), row(DIFF_WIDTH), const((D, D)),
                per_b, const((1, D)), per_b, per_b]
    args = [x, pool_o, hgrn_o, diff_o, w_out, g1, norm_g.reshape(1, D), sc, sh]
    out_specs = [row(D), row(D)]
    out_shape = [jax.ShapeDtypeStruct((B, S, D), F32), jax.ShapeDtypeStruct((B, S, D), BF16)]
    if has_router:
        rw_t, rb = router
        in_specs += [const((N_EXPERTS, D)), const((N_EXPERTS, 1))]
        args += [rw_t, rb]
        out_specs.append(pl.BlockSpec((N_EXPERTS, tm), lambda b, i: (0, b * nt + i)))
        out_shape.append(jax.ShapeDtypeStruct((N_EXPERTS, B * S), F32))
    return pl.pallas_call(
        functools.partial(_out_proj_kernel, has_router=has_router),
        grid=(B, nt),
        in_specs=in_specs, out_specs=out_specs, out_shape=out_shape,
        compiler_params=_params("parallel", "parallel"),
        name="out_proj",
    )(*args)


def _ffn_kernel(h_ref, wg_ref, wu_ref, wd_ref, o_ref):
    @pl.when(pl.program_id(1) == 0)
    def _():
        o_ref[...] = jnp.zeros_like(o_ref)

    h = h_ref[...]
    mid = _silu(_dot(h, wg_ref[...])) * _dot(h, wu_ref[...])
    o_ref[...] += _dot(mid.astype(BF16), wd_ref[...])


def _ffn(h2, wg, wu, wd, tm=1024, tf=256):
    T, D = h2.shape
    F = wg.shape[-1]
    assert T % tm == 0 and F % tf == 0
    return pl.pallas_call(
        _ffn_kernel,
        grid=(T // tm, F // tf),
        in_specs=[
            pl.BlockSpec((tm, D), lambda i, f: (i, 0)),
            pl.BlockSpec((D, tf), lambda i, f: (0, f)),
            pl.BlockSpec((D, tf), lambda i, f: (0, f)),
            pl.BlockSpec((tf, D), lambda i, f: (f, 0)),
        ],
        out_specs=pl.BlockSpec((tm, D), lambda i, f: (i, 0)),
        out_shape=jax.ShapeDtypeStruct((T, D), F32),
        compiler_params=_params("parallel", "arbitrary"),
        name="ffn",
    )(h2, wg, wu, wd)


MOE_SUB = 128
MOE_RANK_BLOCK = 512


def _moe_kernel(h_ref, comb_ref, wg_ref, wu_ref, wd_ref, o_ref, rank_sc, cnt_sc, xc_sc, yc_sc):
    e = pl.program_id(1)
    fi = pl.program_id(2)
    nf = pl.num_programs(2)
    tm = h_ref.shape[0]

    @pl.when((e == 0) & (fi == 0))
    def _():
        o_ref[...] = jnp.zeros_like(o_ref)
        s_i = lax.broadcasted_iota(jnp.int32, (MOE_RANK_BLOCK, MOE_RANK_BLOCK), 0)
        t_i = lax.broadcasted_iota(jnp.int32, (MOE_RANK_BLOCK, MOE_RANK_BLOCK), 1)
        before = (s_i < t_i).astype(BF16)
        base = jnp.zeros((N_EXPERTS, 1), F32)
        for c0 in range(0, tm, MOE_RANK_BLOCK):
            ind = (comb_ref[:, c0:c0 + MOE_RANK_BLOCK] > 0.0).astype(BF16)
            rank_sc[:, c0:c0 + MOE_RANK_BLOCK] = _dot(ind, before) + base
            base = base + jnp.sum(ind.astype(F32), axis=1, keepdims=True)
        for ex in range(N_EXPERTS):
            cnt_sc[ex] = jnp.sum(base[ex:ex + 1, :]).astype(jnp.int32)

    n_sub = (cnt_sc[e] + MOE_SUB - 1) // MOE_SUB
    sel = comb_ref[pl.ds(e, 1), :]
    rank = rank_sc[pl.ds(e, 1), :]
    slot = lax.broadcasted_iota(jnp.int32, (MOE_SUB, tm), 0).astype(F32)

    def onehot(sub):
        hit = (rank == slot + (sub * MOE_SUB).astype(F32)) & (sel > 0.0)
        return hit

    @pl.when(fi == 0)
    def _():
        def gather(sub, carry):
            r0 = pl.multiple_of(sub * MOE_SUB, MOE_SUB)
            g = onehot(sub).astype(BF16)
            xc_sc[pl.ds(r0, MOE_SUB), :] = _dot(g, h_ref[...]).astype(BF16)
            yc_sc[pl.ds(r0, MOE_SUB), :] = jnp.zeros((MOE_SUB, yc_sc.shape[1]), F32)
            return carry
        lax.fori_loop(0, n_sub, gather, 0)

    def expert(sub, carry):
        r0 = pl.multiple_of(sub * MOE_SUB, MOE_SUB)
        xs = xc_sc[pl.ds(r0, MOE_SUB), :]
        mid = _silu(_dot(xs, wg_ref[0])) * _dot(xs, wu_ref[0])
        yc_sc[pl.ds(r0, MOE_SUB), :] += _dot(mid.astype(BF16), wd_ref[0])
        return carry
    lax.fori_loop(0, n_sub, expert, 0)

    @pl.when(fi == nf - 1)
    def _():
        def scatter(sub, carry):
            r0 = pl.multiple_of(sub * MOE_SUB, MOE_SUB)
            hit = onehot(sub)
            weight = jnp.sum(jnp.where(hit, sel, 0.0), axis=1, keepdims=True)
            ys = (yc_sc[pl.ds(r0, MOE_SUB), :] * weight).astype(BF16)
            o_ref[...] += _dot_tn(hit.astype(BF16), ys)
            return carry
        lax.fori_loop(0, n_sub, scatter, 0)


def _moe(h2, comb, wg, wu, wd, tm=2048, tf=256):
    T, D = h2.shape
    F = wg.shape[-1]
    assert T % tm == 0 and F % tf == 0
    return pl.pallas_call(
        _moe_kernel,
        grid=(T // tm, N_EXPERTS, F // tf),
        in_specs=[
            pl.BlockSpec((tm, D), lambda i, e, f: (i, 0)),
            pl.BlockSpec((N_EXPERTS, tm), lambda i, e, f: (0, i)),
            pl.BlockSpec((1, D, tf), lambda i, e, f: (e, 0, f)),
            pl.BlockSpec((1, D, tf), lambda i, e, f: (e, 0, f)),
            pl.BlockSpec((1, tf, D), lambda i, e, f: (e, f, 0)),
        ],
        out_specs=pl.BlockSpec((tm, D), lambda i, e, f: (i, 0)),
        out_shape=jax.ShapeDtypeStruct((T, D), F32),
        scratch_shapes=[
            pltpu.VMEM((N_EXPERTS, tm), F32),
            pltpu.SMEM((N_EXPERTS,), jnp.int32),
            pltpu.VMEM((tm, D), BF16),
            pltpu.VMEM((tm, D), F32),
        ],
        compiler_params=_params("parallel", "arbitrary", "arbitrary"),
        name="moe",
    )(h2, comb, wg, wu, wd)


def _final_kernel(x_ref, f_ref, g2_ref, ng_ref, o_ref):
    x = x_ref[0] + g2_ref[0] * f_ref[0]
    o_ref[0] = x * lax.rsqrt(jnp.mean(x * x, axis=-1, keepdims=True) + RMS_EPS) * ng_ref[...]


def _final(x, f, g2, norm_g, tm=1024):
    B, S, D = x.shape
    row = pl.BlockSpec((1, tm, D), lambda b, i: (b, i, 0))
    return pl.pallas_call(
        _final_kernel,
        grid=(B, S // tm),
        in_specs=[row, row, pl.BlockSpec((1, 1, D), lambda b, i: (b, 0, 0)),
                  pl.BlockSpec((1, D), lambda b, i: (0, 0))],
        out_specs=row,
        out_shape=jax.ShapeDtypeStruct((B, S, D), F32),
        compiler_params=_params("parallel", "parallel"),
        name="final_norm",
    )(x, f, g2, norm_g.reshape(1, D))


def _block_diag(blocks):
    n, r, c = blocks.shape
    out = jnp.zeros((n * r, n * c), blocks.dtype)
    for i in range(n):
        out = out.at[i * r:(i + 1) * r, i * c:(i + 1) * c].set(blocks[i])
    return out


def _trunk(x, ada, prm):
    B, S, D = x.shape
    f = None
    g2 = None
    for l in range(DEPTH):
        sh1, sc1, g1, sh2, sc2, g2_l = [ada[l][:, None, k * D:(k + 1) * D] for k in range(6)]
        x, (u, hg, dq, dk, dv) = _norm_proj(x, f, g2, prm["norm1_g"][l], sc1, sh1, prm["w_in"][l])
        pool_o = _pool(u, prm["pool_bd"][l], prm["pool_scale"][l])
        o_fwd = _hgrn_direction(hg, prm["lbs"][l, 0], prm["ones_bd"], prm["mask_bd"], rev=False)
        hgrn_o = _hgrn_direction(hg, prm["lbs"][l, 1], prm["ones_bd"], prm["mask_bd"], rev=True,
                                 o_fwd=o_fwd, norm_g=prm["hgrn_norm_g"][l])
        diff_o = _diff_attn(dq, dk, dv, prm["diff_scal"][l], prm["diff_norm_g"][l])
        moe = l % 2 == 1
        j = l // 2
        router = (prm["router_wt"][j], prm["router_b"][j]) if moe else None
        res = _out_proj(x, pool_o, hgrn_o, diff_o, prm["w_out"][l], g1, prm["norm2_g"][l], sc2, sh2,
                        router=router)
        x = res[0]
        h2 = res[1].reshape(B * S, D)
        if moe:
            f = _moe(h2, res[2], prm["moe_wg"][j], prm["moe_wu"][j], prm["moe_wd"][j])
        else:
            f = _ffn(h2, prm["ffn_wg"][j], prm["ffn_wu"][j], prm["ffn_wd"][j])
        f = f.reshape(B, S, D)
        g2 = g2_l
    return _final(x, f, g2, prm["final_norm_g"])


def kernel(x_prompt, x_sample, c_prompt, c_sample, ada_w, ada_b, norm1_g, norm2_g, w_in, pool_w, pool_scale, hgrn_lb, hgrn_norm_g, diff_lambda, diff_norm_g, w_out, ffn_w_gate, ffn_w_up, ffn_w_down, router_w, router_b, moe_w_gate, moe_w_up, moe_w_down, final_norm_g):
    nb_p, nb_s = c_prompt.shape[0], c_sample.shape[0]
    rows = -(-(nb_p + nb_s) // 8) * 8
    c_all = jnp.zeros((rows, D_MODEL), F32).at[:nb_p].set(c_prompt).at[nb_p:nb_p + nb_s].set(c_sample)
    ada = _ada(c_all, ada_w, ada_b)

    p_lb = jax.nn.softmax(hgrn_lb.astype(F32), axis=0)
    lbs = jnp.cumsum(p_lb, axis=0) - p_lb[0:1]
    lv = diff_lambda.astype(F32)
    lam_init = jnp.asarray([0.8 - 0.6 * math.exp(-0.3 * l) for l in range(DEPTH)], F32)
    lam = jnp.exp(jnp.sum(lv[:, 0] * lv[:, 1], axis=-1)) - jnp.exp(jnp.sum(lv[:, 2] * lv[:, 3], axis=-1)) + lam_init
    slopes = jnp.asarray([2.0 ** (-8.0 * (h + 1) / DIFF_HEADS) for h in range(DIFF_HEADS)], F32)
    diff_scal = jnp.concatenate(
        [lam[:, None], (1.0 - lam_init)[:, None], jnp.broadcast_to(slopes, (DEPTH, DIFF_HEADS)),
         jnp.zeros((DEPTH, 2), F32)], axis=1)
    head_blocks = jnp.ones((HGRN_WIDTH // HGRN_HEAD_DIM, HGRN_HEAD_DIM, HGRN_HEAD_DIM), F32)
    prm = {
        "norm1_g": norm1_g, "norm2_g": norm2_g, "final_norm_g": final_norm_g,
        "w_in": w_in.astype(BF16), "w_out": w_out.astype(BF16),
        "pool_bd": jnp.stack([_block_diag(pool_w[l]) for l in range(DEPTH)]).astype(BF16),
        "pool_scale": pool_scale,
        "lbs": lbs,
        "hgrn_norm_g": jnp.tile(hgrn_norm_g, (1, HGRN_WIDTH // HGRN_HEAD_DIM)),
        "ones_bd": _block_diag(head_blocks).astype(BF16),
        "mask_bd": _block_diag(head_blocks),
        "diff_scal": diff_scal, "diff_norm_g": diff_norm_g,
        "ffn_wg": ffn_w_gate.astype(BF16), "ffn_wu": ffn_w_up.astype(BF16), "ffn_wd": ffn_w_down.astype(BF16),
        "router_wt": jnp.swapaxes(router_w, 1, 2), "router_b": router_b[:, :, None],
        "moe_wg": moe_w_gate.astype(BF16), "moe_wu": moe_w_up.astype(BF16), "moe_wd": moe_w_down.astype(BF16),
    }
    y_prompt = _trunk(x_prompt, ada[:, :nb_p], prm)
    y_sample = _trunk(x_sample, ada[:, nb_p:nb_p + nb_s], prm)
    return (y_prompt, y_sample)
```

```python
import functools
import math

import jax
import jax.numpy as jnp
from jax import lax
from jax.experimental import pallas as pl
from jax.experimental.pallas import tpu as pltpu

D_MODEL = 1024
DEPTH = 4
POOL_GROUP_DIM = 64
POOL_WIDTH = 256
POOL_WINDOWS = (2, 4, 8, 16)
POOL_HALO = 16
HGRN_HEAD_DIM = 64
HGRN_WIDTH = 256
DIFF_HEADS = 4
DIFF_HEAD_DIM = 64
DIFF_V_DIM = 128
DIFF_QK_WIDTH = 512
DIFF_WIDTH = 512
D_FF = 2816
N_EXPERTS = 8
RMS_EPS = 1e-6

LANES = 128
VMEM_LIMIT_BYTES = 56 * 1024 * 1024

F32 = jnp.float32
BF16 = jnp.bfloat16


def _params(*semantics):
    return pltpu.CompilerParams(dimension_semantics=semantics, vmem_limit_bytes=VMEM_LIMIT_BYTES)


def _split2(a):
    hi = a.astype(BF16)
    lo = (a - hi.astype(F32)).astype(BF16)
    return hi, lo


def _split3(a):
    hi = a.astype(BF16)
    r = a - hi.astype(F32)
    mid = r.astype(BF16)
    lo = (r - mid.astype(F32)).astype(BF16)
    return hi, mid, lo


def _dot(a, b):
    return jnp.dot(a, b, preferred_element_type=F32)


def _dot_nt(a, b):
    return lax.dot_general(a, b, (((1,), (1,)), ((), ())), preferred_element_type=F32)


def _dot_tn(a, b):
    return lax.dot_general(a, b, (((0,), (0,)), ((), ())), preferred_element_type=F32)


def _dot_f32(a, b):
    ah, al = _split2(a)
    bh, bl = _split2(b)
    return _dot(ah, bh) + _dot(ah, bl) + _dot(al, bh)


def _silu(a):
    return a * jax.nn.sigmoid(a)


def _rms_modulate(x, norm_g, scale, shift):
    y = x * lax.rsqrt(jnp.mean(x * x, axis=-1, keepdims=True) + RMS_EPS) * norm_g
    return y * (1.0 + scale) + shift


def _ada_kernel(c_ref, w_ref, b_ref, o_ref):
    o_ref[0] = _dot_f32(_silu(c_ref[...]), w_ref[0]) + b_ref[0]


def _ada(c_all, ada_w, ada_b):
    rows, d = c_all.shape
    tn = 1024
    n_out = ada_w.shape[-1]
    return pl.pallas_call(
        _ada_kernel,
        grid=(DEPTH, n_out // tn),
        in_specs=[
            pl.BlockSpec((rows, d), lambda l, n: (0, 0)),
            pl.BlockSpec((1, d, tn), lambda l, n: (l, 0, n)),
            pl.BlockSpec((1, 1, tn), lambda l, n: (l, 0, n)),
        ],
        out_specs=pl.BlockSpec((1, rows, tn), lambda l, n: (l, 0, n)),
        out_shape=jax.ShapeDtypeStruct((DEPTH, rows, n_out), F32),
        compiler_params=_params("arbitrary", "arbitrary"),
        name="ada",
    )(c_all, ada_w, ada_b.reshape(DEPTH, 1, n_out))


_PROJ_SPLITS = (
    (0, POOL_WIDTH),
    (POOL_WIDTH, POOL_WIDTH + 5 * HGRN_WIDTH),
    (1536, 2048),
    (2048, 2560),
    (2560, 3072),
)


def _norm_proj_kernel(*refs, has_f):
    if has_f:
        x_ref, f_ref, g2_ref, ng_ref, sc_ref, sh_ref, w_ref, xo_ref = refs[:8]
        outs = refs[8:]
        x = x_ref[0] + g2_ref[0] * f_ref[0]
        xo_ref[0] = x
    else:
        x_ref, ng_ref, sc_ref, sh_ref, w_ref = refs[:5]
        outs = refs[5:]
        x = x_ref[0]
    hb = _rms_modulate(x, ng_ref[...], sc_ref[0], sh_ref[0]).astype(BF16)
    for o_ref, (a, b) in zip(outs, _PROJ_SPLITS):
        o_ref[0] = _dot(hb, w_ref[:, a:b]).astype(BF16)


def _norm_proj(x, f, g2, norm_g, sc, sh, w_in, tm=512):
    B, S, D = x.shape
    has_f = f is not None
    row = pl.BlockSpec((1, tm, D), lambda b, i: (b, i, 0))
    per_b = pl.BlockSpec((1, 1, D), lambda b, i: (b, 0, 0))
    in_specs = [row]
    args = [x]
    if has_f:
        in_specs += [row, per_b]
        args += [f, g2]
    in_specs += [pl.BlockSpec((1, D), lambda b, i: (0, 0)), per_b, per_b,
                 pl.BlockSpec(w_in.shape, lambda b, i: (0, 0))]
    args += [norm_g.reshape(1, D), sc, sh, w_in]
    out_specs, out_shape = [], []
    if has_f:
        out_specs.append(row)
        out_shape.append(jax.ShapeDtypeStruct((B, S, D), F32))
    for a, b_ in _PROJ_SPLITS:
        out_specs.append(pl.BlockSpec((1, tm, b_ - a), lambda b, i: (b, i, 0)))
        out_shape.append(jax.ShapeDtypeStruct((B, S, b_ - a), BF16))
    res = pl.pallas_call(
        functools.partial(_norm_proj_kernel, has_f=has_f),
        grid=(B, S // tm),
        in_specs=in_specs, out_specs=out_specs, out_shape=out_shape,
        compiler_params=_params("parallel", "parallel"),
        name="norm_proj",
    )(*args)
    if has_f:
        return res[0], res[1:]
    return x, res


def _pool_kernel(prev_ref, u_ref, next_ref, w_ref, scale_ref, o_ref, *, seq_len):
    i = pl.program_id(1)
    n = pl.num_programs(1)
    tm = u_ref.shape[1]
    u = u_ref[0].astype(F32)
    prev = jnp.where(i > 0, prev_ref[0].astype(F32), 0.0)
    nxt = jnp.where(i < n - 1, next_ref[0].astype(F32), 0.0)
    ext = jnp.concatenate([prev, u, nxt], axis=0)
    rows = ext.shape[0]
    group = lax.broadcasted_iota(jnp.int32, (tm, POOL_WIDTH), 1) // POOL_GROUP_DIM
    s = ext
    win = jnp.zeros((tm, POOL_WIDTH), F32)
    for gi, w in enumerate(POOL_WINDOWS):
        s = s + pltpu.roll(s, w // 2, axis=0)
        lead = w // 2 - 1
        centred = s if lead == 0 else pltpu.roll(s, rows - lead, axis=0)
        win = jnp.where(group == gi, centred[POOL_HALO:POOL_HALO + tm], win)
    t = i * tm + lax.broadcasted_iota(jnp.int32, (tm, POOL_WIDTH), 0)
    half = jnp.left_shift(1, group)
    count = jnp.minimum(t + half, seq_len) - jnp.maximum(t - half, 0)
    d = win / count.astype(F32) - u
    o_ref[0] = (_dot(d.astype(BF16), w_ref[...]) * scale_ref[...]).astype(BF16)


def _pool(u, w_bd, scale, tm=512):
    B, S, W = u.shape
    hb = tm // POOL_HALO
    nh = S // POOL_HALO
    return pl.pallas_call(
        functools.partial(_pool_kernel, seq_len=S),
        grid=(B, S // tm),
        in_specs=[
            pl.BlockSpec((1, POOL_HALO, W), lambda b, i: (b, jnp.maximum(i * hb - 1, 0), 0)),
            pl.BlockSpec((1, tm, W), lambda b, i: (b, i, 0)),
            pl.BlockSpec((1, POOL_HALO, W), lambda b, i: (b, jnp.minimum((i + 1) * hb, nh - 1), 0)),
            pl.BlockSpec((W, W), lambda b, i: (0, 0)),
            pl.BlockSpec((1, W), lambda b, i: (0, 0)),
        ],
        out_specs=pl.BlockSpec((1, tm, W), lambda b, i: (b, i, 0)),
        out_shape=jax.ShapeDtypeStruct((B, S, W), BF16),
        compiler_params=_params("parallel", "parallel"),
        name="pool",
    )(u, u, u, w_bd, scale.reshape(1, W))


HGRN_SUB = 16
HGRN_GROUP = 128


def _hgrn_kernel(*refs, rev, final):
    if final:
        (q_ref, z_ref, v_ref, lb_ref, ones_ref, mask_ref, of_ref, gate_ref, ng_ref,
         o_ref, st_ref, qe_sc, kt_sc, vb_sc, gam_sc, acc_sc) = refs
    else:
        (q_ref, z_ref, v_ref, lb_ref, ones_ref, mask_ref,
         o_ref, st_ref, qe_sc, kt_sc, vb_sc, gam_sc, acc_sc) = refs
    c = HGRN_SUB
    tt = q_ref.shape[1]
    width = q_ref.shape[2]

    @pl.when(pl.program_id(1) == 0)
    def _():
        st_ref[...] = jnp.zeros_like(st_ref)

    q = _silu(q_ref[0].astype(F32))
    z = z_ref[0].astype(F32)
    v = v_ref[0].astype(F32)
    lb = lb_ref[...]
    f = jnp.maximum(lb + (1.0 - lb) * jax.nn.sigmoid(z), 1e-30)
    g = jnp.log(f)
    kk = (1.0 - lb) * jax.nn.sigmoid(-z)

    r_i = lax.broadcasted_iota(jnp.int32, (HGRN_GROUP, HGRN_GROUP), 0)
    s_i = lax.broadcasted_iota(jnp.int32, (HGRN_GROUP, HGRN_GROUP), 1)
    same = (r_i // c) == (s_i // c)
    tri = (same & ((s_i >= r_i) if rev else (s_i <= r_i))).astype(BF16)
    blk = same.astype(BF16)
    cums, tots = [], []
    for g0 in range(0, tt, HGRN_GROUP):
        parts = _split3(g[g0:g0 + HGRN_GROUP])
        cums.append(sum(_dot(tri, p) for p in parts))
        tots.append(sum(_dot(blk, p) for p in parts))
    cum = jnp.concatenate(cums, axis=0)
    tot = jnp.concatenate(tots, axis=0)

    qe_sc[...] = (q * jnp.exp(cum)).astype(BF16)
    kt_sc[...] = (kk * jnp.exp(tot - cum)).astype(BF16)
    vb_sc[...] = v.astype(BF16)
    gam_sc[...] = jnp.exp(tot)

    pos = lax.broadcasted_iota(jnp.int32, (tt, width), 0) % c
    ones_bd = ones_ref[...]
    step = tt - 1 if rev else 1
    k_sh, v_sh, f_sh = kk, v, f
    decay = jnp.ones_like(f)
    acc = _dot((q * kk).astype(BF16), ones_bd) * v
    for delta in range(1, c):
        k_sh = pltpu.roll(k_sh, step, axis=0)
        v_sh = pltpu.roll(v_sh, step, axis=0)
        decay = decay * f_sh
        f_sh = pltpu.roll(f_sh, step, axis=0)
        valid = (pos + delta <= c - 1) if rev else (pos >= delta)
        p = jnp.where(valid, q * k_sh * decay, 0.0)
        acc = acc + _dot(p.astype(BF16), ones_bd) * v_sh
    acc_sc[...] = acc

    n_sub = tt // c
    mask = mask_ref[...]

    def body(ci, carry):
        idx = (n_sub - 1 - ci) if rev else ci
        r0 = pl.multiple_of(idx * c, c)
        st = st_ref[...]
        acc_sc[pl.ds(r0, c), :] += _dot_nt(qe_sc[pl.ds(r0, c), :], st.astype(BF16))
        kv = _dot_tn(vb_sc[pl.ds(r0, c), :], kt_sc[pl.ds(r0, c), :])
        st_ref[...] = st * gam_sc[pl.ds(r0, 8), :][0:1] + kv * mask
        return carry

    lax.fori_loop(0, n_sub, body, 0, unroll=4)

    o = acc_sc[...]
    if final:
        o = o + of_ref[0]
        ms = sum(_dot(p, ones_bd) for p in _split2(o * o)) * (1.0 / HGRN_HEAD_DIM)
        o = o * lax.rsqrt(ms + RMS_EPS) * ng_ref[...] * _silu(gate_ref[0].astype(F32))
        o_ref[0] = o.astype(BF16)
    else:
        o_ref[0] = o


def _hgrn_direction(hg, lb, ones_bd, mask_bd, rev, o_fwd=None, norm_g=None, tt=256):
    B, S, _ = hg.shape
    W = HGRN_WIDTH
    nt = S // tt
    final = o_fwd is not None
    tile = (lambda j: nt - 1 - j) if rev else (lambda j: j)

    def col(k):
        return pl.BlockSpec((1, tt, W), lambda b, j: (b, tile(j), k))

    const = lambda shape: pl.BlockSpec(shape, lambda b, j: (0, 0))
    in_specs = [col(0), col(2 if rev else 1), col(3), const((1, W)), const((W, W)), const((W, W))]
    args = [hg, hg, hg, lb.reshape(1, W), ones_bd, mask_bd]
    if final:
        in_specs += [pl.BlockSpec((1, tt, W), lambda b, j: (b, tile(j), 0)), col(4), const((1, W))]
        args += [o_fwd, hg, norm_g.reshape(1, W)]
    return pl.pallas_call(
        functools.partial(_hgrn_kernel, rev=rev, final=final),
        grid=(B, nt),
        in_specs=in_specs,
        out_specs=pl.BlockSpec((1, tt, W), lambda b, j: (b, tile(j), 0)),
        out_shape=jax.ShapeDtypeStruct((B, S, W), BF16 if final else F32),
        scratch_shapes=[
            pltpu.VMEM((W, W), F32),
            pltpu.VMEM((tt, W), BF16),
            pltpu.VMEM((tt, W), BF16),
            pltpu.VMEM((tt, W), BF16),
            pltpu.VMEM((tt, W), F32),
            pltpu.VMEM((tt, W), F32),
        ],
        compiler_params=_params("parallel", "arbitrary"),
        name="hgrn_bwd" if rev else "hgrn_fwd",
    )(*args)


LOG2E = 1.4426950408889634
ATTN_COL_RADIX = 32
ATTN_SIDES = 3
ATTN_ROW_CHUNK = 256
ATTN_LAG = 1


def _diff_attn_kernel(scal_ref, q_ref, k_ref, v_ref, kaug_ref, ng_ref, o_ref,
                      qx_sc, kf_sc, vf_sc, bias_sc, s_sc, pmax_sc, m_sc, acc_sc, *, nk):
    h = pl.program_id(1)
    t = pl.program_id(2)
    n_pairs = pl.num_programs(2) - ATTN_LAG
    tq = q_ref.shape[1]
    tk = k_ref.shape[1]
    sigma = scal_ref[2 + h] * LOG2E
    pair = jnp.minimum(t, n_pairs - 1)
    qi = pair // nk
    ki = pair % nk
    chunks = range(0, 2 * tq, ATTN_ROW_CHUNK)

    def sigma_digits(lane_idx, first_lane):
        hi, mid, lo = [p.astype(F32) for p in _split3(jnp.full(lane_idx.shape, sigma, F32))]
        part = (lane_idx - first_lane) % 3
        digits = jnp.where(part == 0, hi, jnp.where(part == 1, mid, lo))
        return jnp.where((lane_idx >= first_lane) & (lane_idx < first_lane + 6), digits, 0.0)

    @pl.when(t == 0)
    def _():
        vcol = lax.broadcasted_iota(jnp.int32, (tk, LANES), 1)
        kf_sc[:, LANES:2 * LANES] = jnp.where(
            vcol < 6, kaug_ref[...].astype(F32), sigma_digits(vcol, 6)).astype(BF16)
        vf_sc[:, LANES:2 * LANES] = jnp.where(vcol == 0, 1.0, 0.0).astype(BF16)
        for r0 in chunks:
            rr = (r0 + lax.broadcasted_iota(jnp.int32, (ATTN_ROW_CHUNK, tk), 0)) % tq
            cc = lax.broadcasted_iota(jnp.int32, (ATTN_ROW_CHUNK, tk), 1)
            bias_sc[r0:r0 + ATTN_ROW_CHUNK] = -sigma * jnp.abs(rr - cc).astype(F32)
        s_sc[1] = jnp.zeros(s_sc.shape[1:], F32)
        pmax_sc[1] = jnp.zeros(pmax_sc.shape[1:], F32)
        m_sc[...] = jnp.zeros(m_sc.shape, F32)
        acc_sc[...] = jnp.zeros(acc_sc.shape, F32)

    @pl.when((ki == 0) & (t < n_pairs))
    def _():
        q = q_ref[0].astype(F32) * (DIFF_HEAD_DIM ** -0.5 * LOG2E)
        lane = lax.broadcasted_iota(jnp.int32, q.shape, 1)
        first = lane < DIFF_HEAD_DIM
        qa = jnp.where(first, q, 0.0).astype(BF16)
        qb = jnp.where(first, 0.0, q).astype(BF16)
        row = lax.broadcasted_iota(jnp.int32, q.shape, 0)
        row_digits = jnp.where(lane < 9, row // ATTN_COL_RADIX * ATTN_COL_RADIX, row % ATTN_COL_RADIX)
        aug = sigma_digits(lane, 0) - jnp.where((lane >= 6) & (lane < 12), row_digits.astype(F32), 0.0)
        for side, a in enumerate((aug, -aug, jnp.zeros_like(aug))):
            qx_sc[side, 0:tq, 0:LANES] = qa
            qx_sc[side, tq:2 * tq, 0:LANES] = qb
            qx_sc[side, 0:tq, LANES:2 * LANES] = a.astype(BF16)
            qx_sc[side, tq:2 * tq, LANES:2 * LANES] = a.astype(BF16)

    n_col = tk // LANES

    kf_sc[:, 0:LANES] = k_ref[0]
    vf_sc[:, 0:LANES] = v_ref[0]
    side = jnp.where(ki < qi, 0, jnp.where(ki > qi, 1, 2))

    def score(overlapping, cur):
        for r0 in chunks:
            rows = slice(r0, r0 + ATTN_ROW_CHUNK)
            s = _dot_nt(qx_sc[side, rows, :], kf_sc[...])
            if overlapping:
                s = s + bias_sc[rows]
            s_sc[cur, rows] = s
            pmax_sc[cur, rows] = functools.reduce(
                jnp.maximum, [s[:, c * LANES:(c + 1) * LANES] for c in range(n_col)])

    def accumulate(old):
        scored_before = jnp.clip(t - 1, 0, n_pairs - 1)
        tile_gap = jnp.abs(scored_before // nk - scored_before % nk)
        tile_const = -sigma * (tile_gap * tq).astype(F32)
        first_key_tile = (t + nk - 1) % nk == 0
        for r0 in chunks:
            rows = slice(r0, r0 + ATTN_ROW_CHUNK)
            m_old = jnp.where(first_key_tile, -jnp.inf, m_sc[rows])
            row_max = jnp.max(pmax_sc[old, rows], axis=-1, keepdims=True) + tile_const
            m_new = jnp.maximum(m_old, row_max)
            alpha = jnp.exp2(m_old - m_new)
            shift = jnp.concatenate([m_new - tile_const] * n_col, axis=1)
            p = jnp.exp2((s_sc[old, rows] - shift).astype(BF16))
            acc_sc[rows] = jnp.concatenate([alpha, alpha], axis=1) * acc_sc[rows] + _dot(p, vf_sc[...])
            m_sc[rows] = m_new

    for parity in (0, 1):
        for overlapping in (False, True):
            @pl.when((t % 2 == parity) & ((side == 2) == overlapping))
            def _():
                accumulate(1 - parity)
                score(overlapping, parity)

    @pl.when((t >= ATTN_LAG) & ((t - ATTN_LAG) % nk == nk - 1))
    def _():
        lam = scal_ref[0]
        out_scale = scal_ref[1]
        acc = acc_sc[...]
        o_all = acc[:, 0:DIFF_V_DIM] / acc[:, DIFF_V_DIM:DIFF_V_DIM + 1]
        o = o_all[0:tq] - lam * o_all[tq:2 * tq]
        o = o * lax.rsqrt(jnp.mean(o * o, axis=-1, keepdims=True) + RMS_EPS)
        o_ref[0] = (o * ng_ref[...] * out_scale).astype(BF16)


def _diff_attn(dq, dk, dv, scal, norm_g, ts=1024):
    B, S, _ = dq.shape
    ts = min(ts, S)
    assert S % ts == 0 and (2 * ts) % ATTN_ROW_CHUNK == 0 and ts <= ATTN_COL_RADIX * ATTN_COL_RADIX
    nk = S // ts
    n_pairs = nk * nk
    col = jnp.arange(ts, dtype=jnp.int32)[:, None]
    lane = jnp.arange(LANES, dtype=jnp.int32)[None, :]
    kaug = jnp.where(lane < 3, col // ATTN_COL_RADIX * ATTN_COL_RADIX,
                     jnp.where(lane < 6, col % ATTN_COL_RADIX, 0)).astype(BF16)

    def scored(t):
        return jnp.minimum(t, n_pairs - 1)

    def applied(t):
        return jnp.clip(t - ATTN_LAG, 0, n_pairs - 1)

    pair_buffer = lambda width, dtype: pltpu.VMEM((2, 2 * ts, width), dtype)
    grid_spec = pltpu.PrefetchScalarGridSpec(
        num_scalar_prefetch=0,
        grid=(B, DIFF_HEADS, n_pairs + ATTN_LAG),
        in_specs=[
            pl.BlockSpec(memory_space=pltpu.SMEM),
            pl.BlockSpec((1, ts, LANES), lambda b, h, t: (b, scored(t) // nk, h)),
            pl.BlockSpec((1, ts, LANES), lambda b, h, t: (b, scored(t) % nk, h)),
            pl.BlockSpec((1, ts, LANES), lambda b, h, t: (b, applied(t) % nk, h)),
            pl.BlockSpec((ts, LANES), lambda b, h, t: (0, 0)),
            pl.BlockSpec((1, DIFF_V_DIM), lambda b, h, t: (0, 0)),
        ],
        out_specs=pl.BlockSpec((1, ts, DIFF_V_DIM), lambda b, h, t: (b, applied(t) // nk, h)),
        scratch_shapes=[
            pltpu.VMEM((ATTN_SIDES, 2 * ts, 2 * LANES), BF16),
            pltpu.VMEM((ts, 2 * LANES), BF16),
            pltpu.VMEM((ts, 2 * LANES), BF16),
            pltpu.VMEM((2 * ts, ts), F32),
            pair_buffer(ts, F32),
            pair_buffer(LANES, F32),
            pltpu.VMEM((2 * ts, LANES), F32),
            pltpu.VMEM((2 * ts, 2 * DIFF_V_DIM), F32),
        ],
    )
    return pl.pallas_call(
        functools.partial(_diff_attn_kernel, nk=nk),
        grid_spec=grid_spec,
        out_shape=jax.ShapeDtypeStruct((B, S, DIFF_WIDTH), BF16),
        compiler_params=_params("parallel", "parallel", "arbitrary"),
        name="diff_attn",
    )(scal, dq, dk, dv, kaug, norm_g.reshape(1, DIFF_V_DIM))


def _out_proj_kernel(*refs, has_router):
    (x_ref, pool_ref, hgrn_ref, diff_ref, w_ref, g1_ref, ng_ref, sc_ref, sh_ref) = refs[:9]
    if has_router:
        rw_ref, rb_ref, xo_ref, h2_ref, comb_ref = refs[9:]
    else:
        xo_ref, h2_ref = refs[9:]
    a, b = POOL_WIDTH, POOL_WIDTH + HGRN_WIDTH
    mixed = (_dot(pool_ref[0], w_ref[0:a, :]) + _dot(hgrn_ref[0], w_ref[a:b, :])
             + _dot(diff_ref[0], w_ref[b:, :]))
    x = x_ref[0] + g1_ref[0] * mixed
    xo_ref[0] = x
    h2 = _rms_modulate(x, ng_ref[...], sc_ref[0], sh_ref[0])
    h2_ref[0] = h2.astype(BF16)
    if has_router:
        hh, hl = _split2(h2)
        wh, wl = _split2(rw_ref[...])
        logits = _dot_nt(wh, hh) + _dot_nt(wh, hl) + _dot_nt(wl, hh) + rb_ref[...]
        e_idx = lax.broadcasted_iota(jnp.int32, logits.shape, 0)
        m1 = jnp.max(logits, axis=0, keepdims=True)
        i1 = jnp.min(jnp.where(logits == m1, e_idx, N_EXPERTS), axis=0, keepdims=True)
        first = e_idx == i1
        rest = jnp.where(first, -jnp.inf, logits)
        m2 = jnp.max(rest, axis=0, keepdims=True)
        i2 = jnp.min(jnp.where(rest == m2, e_idx, N_EXPERTS), axis=0, keepdims=True)
        second = e_idx == i2
        e2 = jnp.exp(m2 - m1)
        p1 = 1.0 / (1.0 + e2)
        comb_ref[...] = jnp.where(first, p1, jnp.where(second, e2 * p1, 0.0))


def _out_proj(x, pool_o, hgrn_o, diff_o, w_out, g1, norm_g, sc, sh, router=None, tm=512):
    B, S, D = x.shape
    nt = S // tm
    has_router = router is not None
    row = lambda w: pl.BlockSpec((1, tm, w), lambda b, i: (b, i, 0))
    per_b = pl.BlockSpec((1, 1, D), lambda b, i: (b, 0, 0))
    const = lambda shape: pl.BlockSpec(shape, lambda b, i: (0,) * len(shape))
    in_specs = [row(D), row(POOL_WIDTH), row(HGRN_WIDTH), row(DIFF_WIDTH), const((D, D)),
                per_b, const((1, D)), per_b, per_b]
    args = [x, pool_o, hgrn_o, diff_o, w_out, g1, norm_g.reshape(1, D), sc, sh]
    out_specs = [row(D), row(D)]
    out_shape = [jax.ShapeDtypeStruct((B, S, D), F32), jax.ShapeDtypeStruct((B, S, D), BF16)]
    if has_router:
        rw_t, rb = router
        in_specs += [const((N_EXPERTS, D)), const((N_EXPERTS, 1))]
        args += [rw_t, rb]
        out_specs.append(pl.BlockSpec((N_EXPERTS, tm), lambda b, i: (0, b * nt + i)))
        out_shape.append(jax.ShapeDtypeStruct((N_EXPERTS, B * S), F32))
    return pl.pallas_call(
        functools.partial(_out_proj_kernel, has_router=has_router),
        grid=(B, nt),
        in_specs=in_specs, out_specs=out_specs, out_shape=out_shape,
        compiler_params=_params("parallel", "parallel"),
        name="out_proj",
    )(*args)


def _ffn_kernel(h_ref, wg_ref, wu_ref, wd_ref, o_ref):
    @pl.when(pl.program_id(1) == 0)
    def _():
        o_ref[...] = jnp.zeros_like(o_ref)

    h = h_ref[...]
    mid = _silu(_dot(h, wg_ref[...])) * _dot(h, wu_ref[...])
    o_ref[...] += _dot(mid.astype(BF16), wd_ref[...])


def _ffn(h2, wg, wu, wd, tm=1024, tf=256):
    T, D = h2.shape
    F = wg.shape[-1]
    assert T % tm == 0 and F % tf == 0
    return pl.pallas_call(
        _ffn_kernel,
        grid=(T // tm, F // tf),
        in_specs=[
            pl.BlockSpec((tm, D), lambda i, f: (i, 0)),
            pl.BlockSpec((D, tf), lambda i, f: (0, f)),
            pl.BlockSpec((D, tf), lambda i, f: (0, f)),
            pl.BlockSpec((tf, D), lambda i, f: (f, 0)),
        ],
        out_specs=pl.BlockSpec((tm, D), lambda i, f: (i, 0)),
        out_shape=jax.ShapeDtypeStruct((T, D), F32),
        compiler_params=_params("parallel", "arbitrary"),
        name="ffn",
    )(h2, wg, wu, wd)


MOE_SUB = 256
MOE_TAIL = 128
MOE_RANK_BLOCK = 512


def _moe_kernel(h_ref, comb_ref, wg_ref, wu_ref, wd_ref, o_ref, rank_sc, cnt_sc, xc_sc, yc_sc):
    e = pl.program_id(1)
    fi = pl.program_id(2)
    nf = pl.num_programs(2)
    tm = h_ref.shape[0]

    @pl.when((e == 0) & (fi == 0))
    def _():
        o_ref[...] = jnp.zeros_like(o_ref)
        s_i = lax.broadcasted_iota(jnp.int32, (MOE_RANK_BLOCK, MOE_RANK_BLOCK), 0)
        t_i = lax.broadcasted_iota(jnp.int32, (MOE_RANK_BLOCK, MOE_RANK_BLOCK), 1)
        before = (s_i < t_i).astype(BF16)
        base = jnp.zeros((N_EXPERTS, 1), F32)
        for c0 in range(0, tm, MOE_RANK_BLOCK):
            ind = (comb_ref[:, c0:c0 + MOE_RANK_BLOCK] > 0.0).astype(BF16)
            rank_sc[:, c0:c0 + MOE_RANK_BLOCK] = _dot(ind, before) + base
            base = base + jnp.sum(ind.astype(F32), axis=1, keepdims=True)
        for ex in range(N_EXPERTS):
            cnt_sc[ex] = jnp.sum(base[ex:ex + 1, :]).astype(jnp.int32)

    n_rows = cnt_sc[e]
    n_full = n_rows // MOE_SUB
    rest = n_rows - n_full * MOE_SUB
    n_sub = n_full + (rest > MOE_TAIL).astype(jnp.int32)
    has_tail = (rest > 0) & (rest <= MOE_TAIL)
    sel = comb_ref[pl.ds(e, 1), :]
    rank = rank_sc[pl.ds(e, 1), :]

    def for_blocks(fn):
        def body(sub, carry):
            fn(pl.multiple_of(sub * MOE_SUB, MOE_SUB), MOE_SUB)
            return carry
        lax.fori_loop(0, n_sub, body, 0)

        @pl.when(has_tail)
        def _():
            fn(pl.multiple_of(n_full * MOE_SUB, MOE_SUB), MOE_TAIL)

    def onehot(r0, size):
        slot = lax.broadcasted_iota(jnp.int32, (size, tm), 0) + r0
        return (rank == slot.astype(F32)) & (sel > 0.0)

    @pl.when(fi == 0)
    def _():
        def gather(r0, size):
            g = onehot(r0, size).astype(BF16)
            xc_sc[pl.ds(r0, size), :] = _dot(g, h_ref[...]).astype(BF16)
            yc_sc[pl.ds(r0, size), :] = jnp.zeros((size, yc_sc.shape[1]), F32)
        for_blocks(gather)

    def expert(r0, size):
        xs = xc_sc[pl.ds(r0, size), :]
        mid = _silu(_dot(xs, wg_ref[0])) * _dot(xs, wu_ref[0])
        yc_sc[pl.ds(r0, size), :] += _dot(mid.astype(BF16), wd_ref[0])
    for_blocks(expert)

    @pl.when(fi == nf - 1)
    def _():
        def scatter(r0, size):
            hit = onehot(r0, size)
            weight = jnp.sum(jnp.where(hit, sel, 0.0), axis=1, keepdims=True)
            ys = (yc_sc[pl.ds(r0, size), :] * weight).astype(BF16)
            o_ref[...] += _dot_tn(hit.astype(BF16), ys)
        for_blocks(scatter)


def _moe(h2, comb, wg, wu, wd, tm=2048, tf=256):
    T, D = h2.shape
    F = wg.shape[-1]
    assert T % tm == 0 and F % tf == 0
    return pl.pallas_call(
        _moe_kernel,
        grid=(T // tm, N_EXPERTS, F // tf),
        in_specs=[
            pl.BlockSpec((tm, D), lambda i, e, f: (i, 0)),
            pl.BlockSpec((N_EXPERTS, tm), lambda i, e, f: (0, i)),
            pl.BlockSpec((1, D, tf), lambda i, e, f: (e, 0, f)),
            pl.BlockSpec((1, D, tf), lambda i, e, f: (e, 0, f)),
            pl.BlockSpec((1, tf, D), lambda i, e, f: (e, f, 0)),
        ],
        out_specs=pl.BlockSpec((tm, D), lambda i, e, f: (i, 0)),
        out_shape=jax.ShapeDtypeStruct((T, D), F32),
        scratch_shapes=[
            pltpu.VMEM((N_EXPERTS, tm), F32),
            pltpu.SMEM((N_EXPERTS,), jnp.int32),
            pltpu.VMEM((tm, D), BF16),
            pltpu.VMEM((tm, D), F32),
        ],
        compiler_params=_params("parallel", "arbitrary", "arbitrary"),
        name="moe",
    )(h2, comb, wg, wu, wd)


def _final_kernel(x_ref, f_ref, g2_ref, ng_ref, o_ref):
    x = x_ref[0] + g2_ref[0] * f_ref[0]
    o_ref[0] = x * lax.rsqrt(jnp.mean(x * x, axis=-1, keepdims=True) + RMS_EPS) * ng_ref[...]


def _final(x, f, g2, norm_g, tm=1024):
    B, S, D = x.shape
    row = pl.BlockSpec((1, tm, D), lambda b, i: (b, i, 0))
    return pl.pallas_call(
        _final_kernel,
        grid=(B, S // tm),
        in_specs=[row, row, pl.BlockSpec((1, 1, D), lambda b, i: (b, 0, 0)),
                  pl.BlockSpec((1, D), lambda b, i: (0, 0))],
        out_specs=row,
        out_shape=jax.ShapeDtypeStruct((B, S, D), F32),
        compiler_params=_params("parallel", "parallel"),
        name="final_norm",
    )(x, f, g2, norm_g.reshape(1, D))


def _block_diag(blocks):
    n, r, c = blocks.shape
    out = jnp.zeros((n * r, n * c), blocks.dtype)
    for i in range(n):
        out = out.at[i * r:(i + 1) * r, i * c:(i + 1) * c].set(blocks[i])
    return out


def _trunk(x, ada, prm):
    B, S, D = x.shape
    f = None
    g2 = None
    for l in range(DEPTH):
        sh1, sc1, g1, sh2, sc2, g2_l = [ada[l][:, None, k * D:(k + 1) * D] for k in range(6)]
        x, (u, hg, dq, dk, dv) = _norm_proj(x, f, g2, prm["norm1_g"][l], sc1, sh1, prm["w_in"][l])
        pool_o = _pool(u, prm["pool_bd"][l], prm["pool_scale"][l])
        o_fwd = _hgrn_direction(hg, prm["lbs"][l, 0], prm["ones_bd"], prm["mask_bd"], rev=False)
        hgrn_o = _hgrn_direction(hg, prm["lbs"][l, 1], prm["ones_bd"], prm["mask_bd"], rev=True,
                                 o_fwd=o_fwd, norm_g=prm["hgrn_norm_g"][l])
        diff_o = _diff_attn(dq, dk, dv, prm["diff_scal"][l], prm["diff_norm_g"][l])
        moe = l % 2 == 1
        j = l // 2
        router = (prm["router_wt"][j], prm["router_b"][j]) if moe else None
        res = _out_proj(x, pool_o, hgrn_o, diff_o, prm["w_out"][l], g1, prm["norm2_g"][l], sc2, sh2,
                        router=router)
        x = res[0]
        h2 = res[1].reshape(B * S, D)
        if moe:
            f = _moe(h2, res[2], prm["moe_wg"][j], prm["moe_wu"][j], prm["moe_wd"][j])
        else:
            f = _ffn(h2, prm["ffn_wg"][j], prm["ffn_wu"][j], prm["ffn_wd"][j])
        f = f.reshape(B, S, D)
        g2 = g2_l
    return _final(x, f, g2, prm["final_norm_g"])


def kernel(x_prompt, x_sample, c_prompt, c_sample, ada_w, ada_b, norm1_g, norm2_g, w_in, pool_w, pool_scale, hgrn_lb, hgrn_norm_g, diff_lambda, diff_norm_g, w_out, ffn_w_gate, ffn_w_up, ffn_w_down, router_w, router_b, moe_w_gate, moe_w_up, moe_w_down, final_norm_g):
    nb_p, nb_s = c_prompt.shape[0], c_sample.shape[0]
    rows = -(-(nb_p + nb_s) // 8) * 8
    c_all = jnp.zeros((rows, D_MODEL), F32).at[:nb_p].set(c_prompt).at[nb_p:nb_p + nb_s].set(c_sample)
    ada = _ada(c_all, ada_w, ada_b)

    p_lb = jax.nn.softmax(hgrn_lb.astype(F32), axis=0)
    lbs = jnp.cumsum(p_lb, axis=0) - p_lb[0:1]
    lv = diff_lambda.astype(F32)
    lam_init = jnp.asarray([0.8 - 0.6 * math.exp(-0.3 * l) for l in range(DEPTH)], F32)
    lam = jnp.exp(jnp.sum(lv[:, 0] * lv[:, 1], axis=-1)) - jnp.exp(jnp.sum(lv[:, 2] * lv[:, 3], axis=-1)) + lam_init
    slopes = jnp.asarray([2.0 ** (-8.0 * (h + 1) / DIFF_HEADS) for h in range(DIFF_HEADS)], F32)
    diff_scal = jnp.concatenate(
        [lam[:, None], (1.0 - lam_init)[:, None], jnp.broadcast_to(slopes, (DEPTH, DIFF_HEADS)),
         jnp.zeros((DEPTH, 2), F32)], axis=1)
    head_blocks = jnp.ones((HGRN_WIDTH // HGRN_HEAD_DIM, HGRN_HEAD_DIM, HGRN_HEAD_DIM), F32)
    prm = {
        "norm1_g": norm1_g, "norm2_g": norm2_g, "final_norm_g": final_norm_g,
        "w_in": w_in.astype(BF16), "w_out": w_out.astype(BF16),
        "pool_bd": jnp.stack([_block_diag(pool_w[l]) for l in range(DEPTH)]).astype(BF16),
        "pool_scale": pool_scale,
        "lbs": lbs,
        "hgrn_norm_g": jnp.tile(hgrn_norm_g, (1, HGRN_WIDTH // HGRN_HEAD_DIM)),
        "ones_bd": _block_diag(head_blocks).astype(BF16),
        "mask_bd": _block_diag(head_blocks),
        "diff_scal": diff_scal, "diff_norm_g": diff_norm_g,
        "ffn_wg": ffn_w_gate.astype(BF16), "ffn_wu": ffn_w_up.astype(BF16), "ffn_wd": ffn_w_down.astype(BF16),
        "router_wt": jnp.swapaxes(router_w, 1, 2), "router_b": router_b[:, :, None],
        "moe_wg": moe_w_gate.astype(BF16), "moe_wu": moe_w_up.astype(BF16), "moe_wd": moe_w_down.astype(BF16),
    }
    y_prompt = _trunk(x_prompt, ada[:, :nb_p], prm)
    y_sample = _trunk(x_sample, ada[:, nb_p:nb_p + nb_s], prm)
    return (y_prompt, y_sample)
```

```python
import functools
import math

import jax
import jax.numpy as jnp
from jax import lax
from jax.experimental import pallas as pl
from jax.experimental.pallas import tpu as pltpu

D_MODEL = 1024
DEPTH = 4
POOL_GROUP_DIM = 64
POOL_WIDTH = 256
POOL_WINDOWS = (2, 4, 8, 16)
POOL_HALO = 16
HGRN_HEAD_DIM = 64
HGRN_WIDTH = 256
DIFF_HEADS = 4
DIFF_HEAD_DIM = 64
DIFF_V_DIM = 128
DIFF_QK_WIDTH = 512
DIFF_WIDTH = 512
D_FF = 2816
N_EXPERTS = 8
RMS_EPS = 1e-6

LANES = 128
VMEM_LIMIT_BYTES = 56 * 1024 * 1024

F32 = jnp.float32
BF16 = jnp.bfloat16


def _params(*semantics):
    return pltpu.CompilerParams(dimension_semantics=semantics, vmem_limit_bytes=VMEM_LIMIT_BYTES)


def _split2(a):
    hi = a.astype(BF16)
    lo = (a - hi.astype(F32)).astype(BF16)
    return hi, lo


def _split3(a):
    hi = a.astype(BF16)
    r = a - hi.astype(F32)
    mid = r.astype(BF16)
    lo = (r - mid.astype(F32)).astype(BF16)
    return hi, mid, lo


def _dot(a, b):
    return jnp.dot(a, b, preferred_element_type=F32)


def _dot_nt(a, b):
    return lax.dot_general(a, b, (((1,), (1,)), ((), ())), preferred_element_type=F32)


def _dot_tn(a, b):
    return lax.dot_general(a, b, (((0,), (0,)), ((), ())), preferred_element_type=F32)


def _dot_f32(a, b):
    ah, al = _split2(a)
    bh, bl = _split2(b)
    return _dot(ah, bh) + _dot(ah, bl) + _dot(al, bh)


def _silu(a):
    return a * jax.nn.sigmoid(a)


def _rms_modulate(x, norm_g, scale, shift):
    y = x * lax.rsqrt(jnp.mean(x * x, axis=-1, keepdims=True) + RMS_EPS) * norm_g
    return y * (1.0 + scale) + shift


def _ada_kernel(c_ref, w_ref, b_ref, o_ref):
    o_ref[0] = _dot_f32(_silu(c_ref[...]), w_ref[0]) + b_ref[0]


def _ada(c_all, ada_w, ada_b):
    rows, d = c_all.shape
    tn = 1024
    n_out = ada_w.shape[-1]
    return pl.pallas_call(
        _ada_kernel,
        grid=(DEPTH, n_out // tn),
        in_specs=[
            pl.BlockSpec((rows, d), lambda l, n: (0, 0)),
            pl.BlockSpec((1, d, tn), lambda l, n: (l, 0, n)),
            pl.BlockSpec((1, 1, tn), lambda l, n: (l, 0, n)),
        ],
        out_specs=pl.BlockSpec((1, rows, tn), lambda l, n: (l, 0, n)),
        out_shape=jax.ShapeDtypeStruct((DEPTH, rows, n_out), F32),
        compiler_params=_params("arbitrary", "arbitrary"),
        name="ada",
    )(c_all, ada_w, ada_b.reshape(DEPTH, 1, n_out))


_PROJ_SPLITS = (
    (0, POOL_WIDTH),
    (POOL_WIDTH, POOL_WIDTH + 5 * HGRN_WIDTH),
    (1536, 2048),
    (2048, 2560),
    (2560, 3072),
)
_PROJ_QK = (2, 3)


def _norm_proj_kernel(*refs, has_f):
    if has_f:
        x_ref, f_ref, g2_ref, ng_ref, sc_ref, sh_ref, w_ref, seg_ref, xo_ref = refs[:9]
        outs = refs[9:]
        x = x_ref[0] + g2_ref[0] * f_ref[0].astype(F32)
        xo_ref[0] = x
    else:
        x_ref, ng_ref, sc_ref, sh_ref, w_ref, seg_ref = refs[:6]
        outs = refs[6:]
        x = x_ref[0]
    hb = _rms_modulate(x, ng_ref[...], sc_ref[0], sh_ref[0]).astype(BF16)
    norms = []
    for k, (o_ref, (a, b)) in enumerate(zip(outs, _PROJ_SPLITS)):
        out = _dot(hb, w_ref[:, a:b]).astype(BF16)
        o_ref[0] = out
        if k in _PROJ_QK:
            sq = out.astype(F32)
            per_row = _dot((sq * sq).astype(BF16), seg_ref[...])
            norms.append(jnp.max(per_row, axis=0, keepdims=True))
    outs[-1][0, 0] = jnp.concatenate(norms, axis=0)


def _norm_proj(x, f, g2, norm_g, sc, sh, w_in, tm=512):
    B, S, D = x.shape
    has_f = f is not None
    row = pl.BlockSpec((1, tm, D), lambda b, i: (b, i, 0))
    per_b = pl.BlockSpec((1, 1, D), lambda b, i: (b, 0, 0))
    in_specs = [row]
    args = [x]
    if has_f:
        in_specs += [row, per_b]
        args += [f, g2]
    seg = (jnp.arange(DIFF_QK_WIDTH)[:, None] // DIFF_HEAD_DIM == jnp.arange(LANES)[None, :]).astype(BF16)
    in_specs += [pl.BlockSpec((1, D), lambda b, i: (0, 0)), per_b, per_b,
                 pl.BlockSpec(w_in.shape, lambda b, i: (0, 0)),
                 pl.BlockSpec(seg.shape, lambda b, i: (0, 0))]
    args += [norm_g.reshape(1, D), sc, sh, w_in, seg]
    out_specs, out_shape = [], []
    if has_f:
        out_specs.append(row)
        out_shape.append(jax.ShapeDtypeStruct((B, S, D), F32))
    for a, b_ in _PROJ_SPLITS:
        out_specs.append(pl.BlockSpec((1, tm, b_ - a), lambda b, i: (b, i, 0)))
        out_shape.append(jax.ShapeDtypeStruct((B, S, b_ - a), BF16))
    out_specs.append(pl.BlockSpec((1, 1, len(_PROJ_QK), LANES), lambda b, i: (b, i, 0, 0)))
    out_shape.append(jax.ShapeDtypeStruct((B, S // tm, len(_PROJ_QK), LANES), F32))
    res = pl.pallas_call(
        functools.partial(_norm_proj_kernel, has_f=has_f),
        grid=(B, S // tm),
        in_specs=in_specs, out_specs=out_specs, out_shape=out_shape,
        compiler_params=_params("parallel", "parallel"),
        name="norm_proj",
    )(*args)
    if has_f:
        return res[0], res[1:]
    return x, res


def _tile_norms(sq_norms, seq_len, ts):
    B, n = sq_norms.shape[:2]
    per_head = sq_norms[..., :2 * DIFF_HEADS].reshape(B, n, len(_PROJ_QK), DIFF_HEADS, 2).max(axis=-1)
    ts = min(ts, seq_len)
    per_tile = per_head.reshape(B, seq_len // ts, n // (seq_len // ts), len(_PROJ_QK), DIFF_HEADS).max(axis=2)
    bound = jnp.sqrt(per_tile).transpose(2, 0, 3, 1)
    return bound[0], bound[1]


def _pool_kernel(prev_ref, u_ref, next_ref, w_ref, scale_ref, o_ref, *, seq_len):
    i = pl.program_id(1)
    n = pl.num_programs(1)
    tm = u_ref.shape[1]
    u = u_ref[0].astype(F32)
    prev = jnp.where(i > 0, prev_ref[0].astype(F32), 0.0)
    nxt = jnp.where(i < n - 1, next_ref[0].astype(F32), 0.0)
    ext = jnp.concatenate([prev, u, nxt], axis=0)
    rows = ext.shape[0]
    group = lax.broadcasted_iota(jnp.int32, (tm, POOL_WIDTH), 1) // POOL_GROUP_DIM
    s = ext
    win = jnp.zeros((tm, POOL_WIDTH), F32)
    for gi, w in enumerate(POOL_WINDOWS):
        s = s + pltpu.roll(s, w // 2, axis=0)
        lead = w // 2 - 1
        centred = s if lead == 0 else pltpu.roll(s, rows - lead, axis=0)
        win = jnp.where(group == gi, centred[POOL_HALO:POOL_HALO + tm], win)
    t = i * tm + lax.broadcasted_iota(jnp.int32, (tm, POOL_WIDTH), 0)
    half = jnp.left_shift(1, group)
    count = jnp.minimum(t + half, seq_len) - jnp.maximum(t - half, 0)
    d = win / count.astype(F32) - u
    o_ref[0] = (_dot(d.astype(BF16), w_ref[...]) * scale_ref[...]).astype(BF16)


def _pool(u, w_bd, scale, tm=512):
    B, S, W = u.shape
    hb = tm // POOL_HALO
    nh = S // POOL_HALO
    return pl.pallas_call(
        functools.partial(_pool_kernel, seq_len=S),
        grid=(B, S // tm),
        in_specs=[
            pl.BlockSpec((1, POOL_HALO, W), lambda b, i: (b, jnp.maximum(i * hb - 1, 0), 0)),
            pl.BlockSpec((1, tm, W), lambda b, i: (b, i, 0)),
            pl.BlockSpec((1, POOL_HALO, W), lambda b, i: (b, jnp.minimum((i + 1) * hb, nh - 1), 0)),
            pl.BlockSpec((W, W), lambda b, i: (0, 0)),
            pl.BlockSpec((1, W), lambda b, i: (0, 0)),
        ],
        out_specs=pl.BlockSpec((1, tm, W), lambda b, i: (b, i, 0)),
        out_shape=jax.ShapeDtypeStruct((B, S, W), BF16),
        compiler_params=_params("parallel", "parallel"),
        name="pool",
    )(u, u, u, w_bd, scale.reshape(1, W))


HGRN_SUB = 16
HGRN_GROUP = 128


def _hgrn_kernel(*refs, rev, final):
    if final:
        (q_ref, z_ref, v_ref, lb_ref, ones_ref, mask_ref, of_ref, gate_ref, ng_ref,
         o_ref, st_ref, qe_sc, kt_sc, vb_sc, gam_sc, acc_sc) = refs
    else:
        (q_ref, z_ref, v_ref, lb_ref, ones_ref, mask_ref,
         o_ref, st_ref, qe_sc, kt_sc, vb_sc, gam_sc, acc_sc) = refs
    c = HGRN_SUB
    tt = q_ref.shape[1]
    width = q_ref.shape[2]

    @pl.when(pl.program_id(1) == 0)
    def _():
        st_ref[...] = jnp.zeros_like(st_ref)

    q = _silu(q_ref[0].astype(F32))
    z = z_ref[0].astype(F32)
    v = v_ref[0].astype(F32)
    lb = lb_ref[...]
    f = jnp.maximum(lb + (1.0 - lb) * jax.nn.sigmoid(z), 1e-30)
    g = jnp.log(f)
    kk = (1.0 - lb) * jax.nn.sigmoid(-z)

    r_i = lax.broadcasted_iota(jnp.int32, (HGRN_GROUP, HGRN_GROUP), 0)
    s_i = lax.broadcasted_iota(jnp.int32, (HGRN_GROUP, HGRN_GROUP), 1)
    same = (r_i // c) == (s_i // c)
    tri = (same & ((s_i >= r_i) if rev else (s_i <= r_i))).astype(BF16)
    blk = same.astype(BF16)
    cums, tots = [], []
    for g0 in range(0, tt, HGRN_GROUP):
        parts = _split3(g[g0:g0 + HGRN_GROUP])
        cums.append(sum(_dot(tri, p) for p in parts))
        tots.append(sum(_dot(blk, p) for p in parts))
    cum = jnp.concatenate(cums, axis=0)
    tot = jnp.concatenate(tots, axis=0)

    qe_sc[...] = (q * jnp.exp(cum)).astype(BF16)
    kt_sc[...] = (kk * jnp.exp(tot - cum)).astype(BF16)
    vb_sc[...] = v.astype(BF16)
    gam_sc[...] = jnp.exp(tot)

    pos = lax.broadcasted_iota(jnp.int32, (tt, width), 0) % c
    ones_bd = ones_ref[...]
    step = tt - 1 if rev else 1
    kd, v_sh = kk, v
    acc = _dot((q * kk).astype(BF16), ones_bd) * v
    for delta in range(1, c):
        kd = pltpu.roll(kd, step, axis=0) * f
        v_sh = pltpu.roll(v_sh, step, axis=0)
        valid = (pos + delta <= c - 1) if rev else (pos >= delta)
        p = jnp.where(valid, q * kd, 0.0)
        acc = acc + _dot(p.astype(BF16), ones_bd) * v_sh
    acc_sc[...] = acc

    n_sub = tt // c
    mask = mask_ref[...]

    def body(ci, carry):
        idx = (n_sub - 1 - ci) if rev else ci
        r0 = pl.multiple_of(idx * c, c)
        st = st_ref[...]
        acc_sc[pl.ds(r0, c), :] += _dot_nt(qe_sc[pl.ds(r0, c), :], st.astype(BF16))
        kv = _dot_tn(vb_sc[pl.ds(r0, c), :], kt_sc[pl.ds(r0, c), :])
        st_ref[...] = st * gam_sc[pl.ds(r0, 8), :][0:1] + kv * mask
        return carry

    lax.fori_loop(0, n_sub, body, 0, unroll=4)

    o = acc_sc[...]
    if final:
        o = o + of_ref[0]
        ms = sum(_dot(p, ones_bd) for p in _split2(o * o)) * (1.0 / HGRN_HEAD_DIM)
        o = o * lax.rsqrt(ms + RMS_EPS) * ng_ref[...] * _silu(gate_ref[0].astype(F32))
        o_ref[0] = o.astype(BF16)
    else:
        o_ref[0] = o


def _hgrn_direction(hg, lb, ones_bd, mask_bd, rev, o_fwd=None, norm_g=None, tt=256):
    B, S, _ = hg.shape
    W = HGRN_WIDTH
    nt = S // tt
    final = o_fwd is not None
    tile = (lambda j: nt - 1 - j) if rev else (lambda j: j)

    def col(k):
        return pl.BlockSpec((1, tt, W), lambda b, j: (b, tile(j), k))

    const = lambda shape: pl.BlockSpec(shape, lambda b, j: (0, 0))
    in_specs = [col(0), col(2 if rev else 1), col(3), const((1, W)), const((W, W)), const((W, W))]
    args = [hg, hg, hg, lb.reshape(1, W), ones_bd, mask_bd]
    if final:
        in_specs += [pl.BlockSpec((1, tt, W), lambda b, j: (b, tile(j), 0)), col(4), const((1, W))]
        args += [o_fwd, hg, norm_g.reshape(1, W)]
    return pl.pallas_call(
        functools.partial(_hgrn_kernel, rev=rev, final=final),
        grid=(B, nt),
        in_specs=in_specs,
        out_specs=pl.BlockSpec((1, tt, W), lambda b, j: (b, tile(j), 0)),
        out_shape=jax.ShapeDtypeStruct((B, S, W), BF16 if final else F32),
        scratch_shapes=[
            pltpu.VMEM((W, W), F32),
            pltpu.VMEM((tt, W), BF16),
            pltpu.VMEM((tt, W), BF16),
            pltpu.VMEM((tt, W), BF16),
            pltpu.VMEM((tt, W), F32),
            pltpu.VMEM((tt, W), F32),
        ],
        compiler_params=_params("parallel", "arbitrary"),
        name="hgrn_bwd" if rev else "hgrn_fwd",
    )(*args)


LOG2E = 1.4426950408889634
ATTN_COL_RADIX = 32
ATTN_SIDES = 3
ATTN_ROW_CHUNK = 256
ATTN_LAG = 1
ATTN_TILE = 1024


def _diff_attn_kernel(pq_ref, pk_ref, cnt_ref, scal_ref, q_ref, k_ref, v_ref, kaug_ref, ng_ref, o_ref,
                      qx_sc, kf_sc, vf_sc, bias_sc, s_sc, pmax_sc, m_sc, acc_sc):
    h = pl.program_id(1)
    t = pl.program_id(2)
    lists = pl.program_id(0) * pl.num_programs(1) + h
    n_pairs = cnt_ref[lists]
    base = lists * (pl.num_programs(2) - ATTN_LAG)
    tq = q_ref.shape[1]
    tk = k_ref.shape[1]
    sigma = scal_ref[2 + h] * LOG2E

    def pair_at(step):
        i = base + jnp.clip(step, 0, n_pairs - 1)
        return pq_ref[i], pk_ref[i]

    qi, ki = pair_at(t)
    qi_done, ki_done = pair_at(t - 1)
    new_query_tile = (t == 0) | (qi != qi_done)
    done_first_of_tile = (t <= 1) | (qi_done != pair_at(t - 2)[0])
    done_last_of_tile = (t >= 1) & (t <= n_pairs) & ((t == n_pairs) | (qi != qi_done))
    chunks = range(0, 2 * tq, ATTN_ROW_CHUNK)

    def sigma_digits(lane_idx, first_lane):
        hi, mid, lo = [p.astype(F32) for p in _split3(jnp.full(lane_idx.shape, sigma, F32))]
        part = (lane_idx - first_lane) % 3
        digits = jnp.where(part == 0, hi, jnp.where(part == 1, mid, lo))
        return jnp.where((lane_idx >= first_lane) & (lane_idx < first_lane + 6), digits, 0.0)

    @pl.when(t == 0)
    def _():
        vcol = lax.broadcasted_iota(jnp.int32, (tk, LANES), 1)
        kf_sc[:, LANES:2 * LANES] = jnp.where(
            vcol < 6, kaug_ref[...].astype(F32), sigma_digits(vcol, 6)).astype(BF16)
        vf_sc[:, LANES:2 * LANES] = jnp.where(vcol == 0, 1.0, 0.0).astype(BF16)
        for r0 in chunks:
            rr = (r0 + lax.broadcasted_iota(jnp.int32, (ATTN_ROW_CHUNK, tk), 0)) % tq
            cc = lax.broadcasted_iota(jnp.int32, (ATTN_ROW_CHUNK, tk), 1)
            bias_sc[r0:r0 + ATTN_ROW_CHUNK] = -sigma * jnp.abs(rr - cc).astype(F32)
        s_sc[1] = jnp.zeros(s_sc.shape[1:], F32)
        pmax_sc[1] = jnp.zeros(pmax_sc.shape[1:], F32)
        m_sc[...] = jnp.zeros(m_sc.shape, F32)
        acc_sc[...] = jnp.zeros(acc_sc.shape, F32)

    @pl.when(new_query_tile)
    def _():
        q = q_ref[0].astype(F32) * (DIFF_HEAD_DIM ** -0.5 * LOG2E)
        lane = lax.broadcasted_iota(jnp.int32, q.shape, 1)
        first = lane < DIFF_HEAD_DIM
        qa = jnp.where(first, q, 0.0).astype(BF16)
        qb = jnp.where(first, 0.0, q).astype(BF16)
        row = lax.broadcasted_iota(jnp.int32, q.shape, 0)
        row_digits = jnp.where(lane < 9, row // ATTN_COL_RADIX * ATTN_COL_RADIX, row % ATTN_COL_RADIX)
        aug = sigma_digits(lane, 0) - jnp.where((lane >= 6) & (lane < 12), row_digits.astype(F32), 0.0)
        for side, a in enumerate((aug, -aug, jnp.zeros_like(aug))):
            qx_sc[side, 0:tq, 0:LANES] = qa
            qx_sc[side, tq:2 * tq, 0:LANES] = qb
            qx_sc[side, 0:tq, LANES:2 * LANES] = a.astype(BF16)
            qx_sc[side, tq:2 * tq, LANES:2 * LANES] = a.astype(BF16)

    n_col = tk // LANES

    kf_sc[:, 0:LANES] = k_ref[0]
    vf_sc[:, 0:LANES] = v_ref[0]
    side = jnp.where(ki < qi, 0, jnp.where(ki > qi, 1, 2))

    def score(overlapping, cur):
        for r0 in chunks:
            rows = slice(r0, r0 + ATTN_ROW_CHUNK)
            s = _dot_nt(qx_sc[side, rows, :], kf_sc[...])
            if overlapping:
                s = s + bias_sc[rows]
            s_sc[cur, rows] = s
            pmax_sc[cur, rows] = functools.reduce(
                jnp.maximum, [s[:, c * LANES:(c + 1) * LANES] for c in range(n_col)])

    def accumulate(old):
        tile_const = -sigma * (jnp.abs(qi_done - ki_done) * tq).astype(F32)
        for r0 in chunks:
            rows = slice(r0, r0 + ATTN_ROW_CHUNK)
            m_old = jnp.where(done_first_of_tile, -jnp.inf, m_sc[rows])
            row_max = jnp.max(pmax_sc[old, rows], axis=-1, keepdims=True) + tile_const
            m_new = jnp.maximum(m_old, row_max)
            alpha = jnp.exp2(m_old - m_new)
            shift = jnp.concatenate([m_new - tile_const] * n_col, axis=1)
            p = jnp.exp2((s_sc[old, rows] - shift).astype(BF16))
            acc_sc[rows] = jnp.concatenate([alpha, alpha], axis=1) * acc_sc[rows] + _dot(p, vf_sc[...])
            m_sc[rows] = m_new

    for parity in (0, 1):
        for overlapping in (False, True):
            @pl.when((t <= n_pairs) & (t % 2 == parity) & ((side == 2) == overlapping))
            def _():
                accumulate(1 - parity)
                score(overlapping, parity)

    @pl.when(done_last_of_tile)
    def _():
        lam = scal_ref[0]
        out_scale = scal_ref[1]
        acc = acc_sc[...]
        o_all = acc[:, 0:DIFF_V_DIM] / acc[:, DIFF_V_DIM:DIFF_V_DIM + 1]
        o = o_all[0:tq] - lam * o_all[tq:2 * tq]
        o = o * lax.rsqrt(jnp.mean(o * o, axis=-1, keepdims=True) + RMS_EPS)
        o_ref[0] = (o * ng_ref[...] * out_scale).astype(BF16)


ATTN_SKIP_MARGIN = 110.0
ATTN_NORM_SLACK = 1.02
ALIBI_SLOPES = tuple(2.0 ** (-8.0 * (h + 1) / DIFF_HEADS) for h in range(DIFF_HEADS))


def _attn_pair_lists(q_norm, k_norm, ts):
    B, H, n = q_norm.shape
    scale = ATTN_NORM_SLACK * DIFF_HEAD_DIM ** -0.5
    i = jnp.arange(n)
    gap = jnp.abs(i[:, None] - i[None, :])
    closest = jnp.where(gap == 0, 0, (gap - 1) * ts + 1).astype(F32)
    slopes = jnp.asarray(ALIBI_SLOPES, F32)[None, :, None, None]
    best = scale * q_norm[:, :, :, None] * k_norm[:, :, None, :] - slopes * closest[None, None]
    floor = -scale * q_norm * k_norm
    active = (best >= floor[:, :, :, None] - ATTN_SKIP_MARGIN) | (gap == 0)[None, None]
    flat = active.reshape(B, H, n * n)
    idx = jnp.arange(n * n, dtype=jnp.int32)
    order = jnp.argsort(jnp.where(flat, idx, idx + n * n), axis=-1).astype(jnp.int32)
    count = jnp.sum(flat, axis=-1).astype(jnp.int32)
    order = jnp.take_along_axis(order, jnp.minimum(idx[None, None], count[..., None] - 1), axis=-1)
    return (order // n).reshape(-1), (order % n).reshape(-1), count.reshape(-1)


def _diff_attn(dq, dk, dv, scal, norm_g, q_norm, k_norm, ts=ATTN_TILE):
    B, S, _ = dq.shape
    ts = min(ts, S)
    assert S % ts == 0 and (2 * ts) % ATTN_ROW_CHUNK == 0 and ts <= ATTN_COL_RADIX * ATTN_COL_RADIX
    nk = S // ts
    n_pairs = nk * nk
    pair_q, pair_k, count = _attn_pair_lists(q_norm, k_norm, ts)
    col = jnp.arange(ts, dtype=jnp.int32)[:, None]
    lane = jnp.arange(LANES, dtype=jnp.int32)[None, :]
    kaug = jnp.where(lane < 3, col // ATTN_COL_RADIX * ATTN_COL_RADIX,
                     jnp.where(lane < 6, col % ATTN_COL_RADIX, 0)).astype(BF16)

    def tile_of(pairs, lag):
        def index_map(b, h, t, pq, pk, cnt):
            lists = b * DIFF_HEADS + h
            return b, pairs(pq, pk)[lists * n_pairs + jnp.clip(t - lag, 0, cnt[lists] - 1)], h
        return index_map

    query_tile = lambda pq, pk: pq
    key_tile = lambda pq, pk: pk
    const = lambda *shape: pl.BlockSpec(shape, lambda b, h, t, pq, pk, cnt: (0,) * len(shape))
    pair_buffer = lambda width, dtype: pltpu.VMEM((2, 2 * ts, width), dtype)
    grid_spec = pltpu.PrefetchScalarGridSpec(
        num_scalar_prefetch=3,
        grid=(B, DIFF_HEADS, n_pairs + ATTN_LAG),
        in_specs=[
            pl.BlockSpec(memory_space=pltpu.SMEM),
            pl.BlockSpec((1, ts, LANES), tile_of(query_tile, 0)),
            pl.BlockSpec((1, ts, LANES), tile_of(key_tile, 0)),
            pl.BlockSpec((1, ts, LANES), tile_of(key_tile, ATTN_LAG)),
            const(ts, LANES),
            const(1, DIFF_V_DIM),
        ],
        out_specs=pl.BlockSpec((1, ts, DIFF_V_DIM), tile_of(query_tile, ATTN_LAG)),
        scratch_shapes=[
            pltpu.VMEM((ATTN_SIDES, 2 * ts, 2 * LANES), BF16),
            pltpu.VMEM((ts, 2 * LANES), BF16),
            pltpu.VMEM((ts, 2 * LANES), BF16),
            pltpu.VMEM((2 * ts, ts), F32),
            pair_buffer(ts, F32),
            pair_buffer(LANES, F32),
            pltpu.VMEM((2 * ts, LANES), F32),
            pltpu.VMEM((2 * ts, 2 * DIFF_V_DIM), F32),
        ],
    )
    return pl.pallas_call(
        _diff_attn_kernel,
        grid_spec=grid_spec,
        out_shape=jax.ShapeDtypeStruct((B, S, DIFF_WIDTH), BF16),
        compiler_params=_params("parallel", "parallel", "arbitrary"),
        name="diff_attn",
    )(pair_q, pair_k, count, scal, dq, dk, dv, kaug, norm_g.reshape(1, DIFF_V_DIM))


def _out_proj_kernel(*refs, has_router):
    (x_ref, pool_ref, hgrn_ref, diff_ref, w_ref, g1_ref, ng_ref, sc_ref, sh_ref) = refs[:9]
    if has_router:
        rw_ref, rb_ref, xo_ref, h2_ref, comb_ref = refs[9:]
    else:
        xo_ref, h2_ref = refs[9:]
    a, b = POOL_WIDTH, POOL_WIDTH + HGRN_WIDTH
    mixed = (_dot(pool_ref[0], w_ref[0:a, :]) + _dot(hgrn_ref[0], w_ref[a:b, :])
             + _dot(diff_ref[0], w_ref[b:, :]))
    x = x_ref[0] + g1_ref[0] * mixed
    xo_ref[0] = x
    h2 = _rms_modulate(x, ng_ref[...], sc_ref[0], sh_ref[0])
    h2_ref[0] = h2.astype(BF16)
    if has_router:
        hh, hl = _split2(h2)
        wh, wl = _split2(rw_ref[...])
        logits = _dot_nt(wh, hh) + _dot_nt(wh, hl) + _dot_nt(wl, hh) + rb_ref[...]
        e_idx = lax.broadcasted_iota(jnp.int32, logits.shape, 0)
        m1 = jnp.max(logits, axis=0, keepdims=True)
        i1 = jnp.min(jnp.where(logits == m1, e_idx, N_EXPERTS), axis=0, keepdims=True)
        first = e_idx == i1
        rest = jnp.where(first, -jnp.inf, logits)
        m2 = jnp.max(rest, axis=0, keepdims=True)
        i2 = jnp.min(jnp.where(rest == m2, e_idx, N_EXPERTS), axis=0, keepdims=True)
        second = e_idx == i2
        e2 = jnp.exp(m2 - m1)
        p1 = 1.0 / (1.0 + e2)
        comb_ref[...] = jnp.where(first, p1, jnp.where(second, e2 * p1, 0.0))


def _out_proj(x, pool_o, hgrn_o, diff_o, w_out, g1, norm_g, sc, sh, router=None, tm=512):
    B, S, D = x.shape
    nt = S // tm
    has_router = router is not None
    row = lambda w: pl.BlockSpec((1, tm, w), lambda b, i: (b, i, 0))
    per_b = pl.BlockSpec((1, 1, D), lambda b, i: (b, 0, 0))
    const = lambda shape: pl.BlockSpec(shape, lambda b, i: (0,) * len(shape))
    in_specs = [row(D), row(POOL_WIDTH), row(HGRN_WIDTH), row(DIFF_WIDTH), const((D, D)),
                per_b, const((1, D)), per_b, per_b]
    args = [x, pool_o, hgrn_o, diff_o, w_out, g1, norm_g.reshape(1, D), sc, sh]
    out_specs = [row(D), row(D)]
    out_shape = [jax.ShapeDtypeStruct((B, S, D), F32), jax.ShapeDtypeStruct((B, S, D), BF16)]
    if has_router:
        rw_t, rb = router
        in_specs += [const((N_EXPERTS, D)), const((N_EXPERTS, 1))]
        args += [rw_t, rb]
        out_specs.append(pl.BlockSpec((N_EXPERTS, tm), lambda b, i: (0, b * nt + i)))
        out_shape.append(jax.ShapeDtypeStruct((N_EXPERTS, B * S), F32))
    return pl.pallas_call(
        functools.partial(_out_proj_kernel, has_router=has_router),
        grid=(B, nt),
        in_specs=in_specs, out_specs=out_specs, out_shape=out_shape,
        compiler_params=_params("parallel", "parallel"),
        name="out_proj",
    )(*args)


def _ffn_kernel(h_ref, wg_ref, wu_ref, wd_ref, o_ref):
    @pl.when(pl.program_id(1) == 0)
    def _():
        o_ref[...] = jnp.zeros_like(o_ref)

    h = h_ref[...]
    mid = _silu(_dot(h, wg_ref[...])) * _dot(h, wu_ref[...])
    o_ref[...] += _dot(mid.astype(BF16), wd_ref[...])


def _ffn(h2, wg, wu, wd, tm=1024, tf=256):
    T, D = h2.shape
    F = wg.shape[-1]
    assert T % tm == 0 and F % tf == 0
    return pl.pallas_call(
        _ffn_kernel,
        grid=(T // tm, F // tf),
        in_specs=[
            pl.BlockSpec((tm, D), lambda i, f: (i, 0)),
            pl.BlockSpec((D, tf), lambda i, f: (0, f)),
            pl.BlockSpec((D, tf), lambda i, f: (0, f)),
            pl.BlockSpec((tf, D), lambda i, f: (f, 0)),
        ],
        out_specs=pl.BlockSpec((tm, D), lambda i, f: (i, 0)),
        out_shape=jax.ShapeDtypeStruct((T, D), F32),
        compiler_params=_params("parallel", "arbitrary"),
        name="ffn",
    )(h2, wg, wu, wd)


MOE_SUB = 256
MOE_TAIL = 128
MOE_RANK_BLOCK = 512


def _moe_kernel(h_ref, comb_ref, wg_ref, wu_ref, wd_ref, o_ref, rank_sc, cnt_sc, xc_sc, yc_sc):
    e = pl.program_id(1)
    fi = pl.program_id(2)
    nf = pl.num_programs(2)
    tm = h_ref.shape[0]

    @pl.when((e == 0) & (fi == 0))
    def _():
        o_ref[...] = jnp.zeros_like(o_ref)
        s_i = lax.broadcasted_iota(jnp.int32, (MOE_RANK_BLOCK, MOE_RANK_BLOCK), 0)
        t_i = lax.broadcasted_iota(jnp.int32, (MOE_RANK_BLOCK, MOE_RANK_BLOCK), 1)
        before = (s_i < t_i).astype(BF16)
        base = jnp.zeros((N_EXPERTS, 1), F32)
        for c0 in range(0, tm, MOE_RANK_BLOCK):
            ind = (comb_ref[:, c0:c0 + MOE_RANK_BLOCK] > 0.0).astype(BF16)
            rank_sc[:, c0:c0 + MOE_RANK_BLOCK] = _dot(ind, before) + base
            base = base + jnp.sum(ind.astype(F32), axis=1, keepdims=True)
        for ex in range(N_EXPERTS):
            cnt_sc[ex] = jnp.sum(base[ex:ex + 1, :]).astype(jnp.int32)

    n_rows = cnt_sc[e]
    n_full = n_rows // MOE_SUB
    rest = n_rows - n_full * MOE_SUB
    n_sub = n_full + (rest > MOE_TAIL).astype(jnp.int32)
    has_tail = (rest > 0) & (rest <= MOE_TAIL)
    sel = comb_ref[pl.ds(e, 1), :]
    rank = rank_sc[pl.ds(e, 1), :]

    def for_blocks(fn):
        def body(sub, carry):
            fn(pl.multiple_of(sub * MOE_SUB, MOE_SUB), MOE_SUB)
            return carry
        lax.fori_loop(0, n_sub, body, 0)

        @pl.when(has_tail)
        def _():
            fn(pl.multiple_of(n_full * MOE_SUB, MOE_SUB), MOE_TAIL)

    def onehot(r0, size):
        slot = lax.broadcasted_iota(jnp.int32, (size, tm), 0) + r0
        return (rank == slot.astype(F32)) & (sel > 0.0)

    @pl.when(fi == 0)
    def _():
        def gather(r0, size):
            g = onehot(r0, size).astype(BF16)
            xc_sc[pl.ds(r0, size), :] = _dot(g, h_ref[...]).astype(BF16)
            yc_sc[pl.ds(r0, size), :] = jnp.zeros((size, yc_sc.shape[1]), F32)
        for_blocks(gather)

    def expert(r0, size):
        xs = xc_sc[pl.ds(r0, size), :]
        mid = _silu(_dot(xs, wg_ref[0])) * _dot(xs, wu_ref[0])
        yc_sc[pl.ds(r0, size), :] += _dot(mid.astype(BF16), wd_ref[0])
    for_blocks(expert)

    @pl.when(fi == nf - 1)
    def _():
        def scatter(r0, size):
            hit = onehot(r0, size)
            weight = jnp.sum(jnp.where(hit, sel, 0.0), axis=1, keepdims=True)
            ys = (yc_sc[pl.ds(r0, size), :] * weight).astype(BF16)
            o_ref[...] = (o_ref[...].astype(F32) + _dot_tn(hit.astype(BF16), ys)).astype(o_ref.dtype)
        for_blocks(scatter)


def _moe(h2, comb, wg, wu, wd, tm=2048, tf=1408):
    T, D = h2.shape
    F = wg.shape[-1]
    assert T % tm == 0 and F % tf == 0
    return pl.pallas_call(
        _moe_kernel,
        grid=(T // tm, N_EXPERTS, F // tf),
        in_specs=[
            pl.BlockSpec((tm, D), lambda i, e, f: (i, 0)),
            pl.BlockSpec((N_EXPERTS, tm), lambda i, e, f: (0, i)),
            pl.BlockSpec((1, D, tf), lambda i, e, f: (e, 0, f)),
            pl.BlockSpec((1, D, tf), lambda i, e, f: (e, 0, f)),
            pl.BlockSpec((1, tf, D), lambda i, e, f: (e, f, 0)),
        ],
        out_specs=pl.BlockSpec((tm, D), lambda i, e, f: (i, 0)),
        out_shape=jax.ShapeDtypeStruct((T, D), BF16),
        scratch_shapes=[
            pltpu.VMEM((N_EXPERTS, tm), F32),
            pltpu.SMEM((N_EXPERTS,), jnp.int32),
            pltpu.VMEM((tm, D), BF16),
            pltpu.VMEM((tm, D), F32),
        ],
        compiler_params=_params("parallel", "arbitrary", "arbitrary"),
        name="moe",
    )(h2, comb, wg, wu, wd)


def _final_kernel(x_ref, f_ref, g2_ref, ng_ref, o_ref):
    x = x_ref[0] + g2_ref[0] * f_ref[0].astype(F32)
    o_ref[0] = x * lax.rsqrt(jnp.mean(x * x, axis=-1, keepdims=True) + RMS_EPS) * ng_ref[...]


def _final(x, f, g2, norm_g, tm=1024):
    B, S, D = x.shape
    row = pl.BlockSpec((1, tm, D), lambda b, i: (b, i, 0))
    return pl.pallas_call(
        _final_kernel,
        grid=(B, S // tm),
        in_specs=[row, row, pl.BlockSpec((1, 1, D), lambda b, i: (b, 0, 0)),
                  pl.BlockSpec((1, D), lambda b, i: (0, 0))],
        out_specs=row,
        out_shape=jax.ShapeDtypeStruct((B, S, D), F32),
        compiler_params=_params("parallel", "parallel"),
        name="final_norm",
    )(x, f, g2, norm_g.reshape(1, D))


def _block_diag(blocks):
    n, r, c = blocks.shape
    out = jnp.zeros((n * r, n * c), blocks.dtype)
    for i in range(n):
        out = out.at[i * r:(i + 1) * r, i * c:(i + 1) * c].set(blocks[i])
    return out


def _trunk(x, ada, prm):
    B, S, D = x.shape
    f = None
    g2 = None
    for l in range(DEPTH):
        sh1, sc1, g1, sh2, sc2, g2_l = [ada[l][:, None, k * D:(k + 1) * D] for k in range(6)]
        x, (u, hg, dq, dk, dv, sq_norms) = _norm_proj(x, f, g2, prm["norm1_g"][l], sc1, sh1, prm["w_in"][l])
        q_norm, k_norm = _tile_norms(sq_norms, S, ATTN_TILE)
        pool_o = _pool(u, prm["pool_bd"][l], prm["pool_scale"][l])
        o_fwd = _hgrn_direction(hg, prm["lbs"][l, 0], prm["ones_bd"], prm["mask_bd"], rev=False)
        hgrn_o = _hgrn_direction(hg, prm["lbs"][l, 1], prm["ones_bd"], prm["mask_bd"], rev=True,
                                 o_fwd=o_fwd, norm_g=prm["hgrn_norm_g"][l])
        diff_o = _diff_attn(dq, dk, dv, prm["diff_scal"][l], prm["diff_norm_g"][l], q_norm, k_norm)
        moe = l % 2 == 1
        j = l // 2
        router = (prm["router_wt"][j], prm["router_b"][j]) if moe else None
        res = _out_proj(x, pool_o, hgrn_o, diff_o, prm["w_out"][l], g1, prm["norm2_g"][l], sc2, sh2,
                        router=router)
        x = res[0]
        h2 = res[1].reshape(B * S, D)
        if moe:
            f = _moe(h2, res[2], prm["moe_wg"][j], prm["moe_wu"][j], prm["moe_wd"][j])
        else:
            f = _ffn(h2, prm["ffn_wg"][j], prm["ffn_wu"][j], prm["ffn_wd"][j])
        f = f.reshape(B, S, D)
        g2 = g2_l
    return _final(x, f, g2, prm["final_norm_g"])


def kernel(x_prompt, x_sample, c_prompt, c_sample, ada_w, ada_b, norm1_g, norm2_g, w_in, pool_w, pool_scale, hgrn_lb, hgrn_norm_g, diff_lambda, diff_norm_g, w_out, ffn_w_gate, ffn_w_up, ffn_w_down, router_w, router_b, moe_w_gate, moe_w_up, moe_w_down, final_norm_g):
    nb_p, nb_s = c_prompt.shape[0], c_sample.shape[0]
    rows = -(-(nb_p + nb_s) // 8) * 8
    c_all = jnp.zeros((rows, D_MODEL), F32).at[:nb_p].set(c_prompt).at[nb_p:nb_p + nb_s].set(c_sample)
    ada = _ada(c_all, ada_w, ada_b)

    p_lb = jax.nn.softmax(hgrn_lb.astype(F32), axis=0)
    lbs = jnp.cumsum(p_lb, axis=0) - p_lb[0:1]
    lv = diff_lambda.astype(F32)
    lam_init = jnp.asarray([0.8 - 0.6 * math.exp(-0.3 * l) for l in range(DEPTH)], F32)
    lam = jnp.exp(jnp.sum(lv[:, 0] * lv[:, 1], axis=-1)) - jnp.exp(jnp.sum(lv[:, 2] * lv[:, 3], axis=-1)) + lam_init
    slopes = jnp.asarray([2.0 ** (-8.0 * (h + 1) / DIFF_HEADS) for h in range(DIFF_HEADS)], F32)
    diff_scal = jnp.concatenate(
        [lam[:, None], (1.0 - lam_init)[:, None], jnp.broadcast_to(slopes, (DEPTH, DIFF_HEADS)),
         jnp.zeros((DEPTH, 2), F32)], axis=1)
    head_blocks = jnp.ones((HGRN_WIDTH // HGRN_HEAD_DIM, HGRN_HEAD_DIM, HGRN_HEAD_DIM), F32)
    prm = {
        "norm1_g": norm1_g, "norm2_g": norm2_g, "final_norm_g": final_norm_g,
        "w_in": w_in.astype(BF16), "w_out": w_out.astype(BF16),
        "pool_bd": jnp.stack([_block_diag(pool_w[l]) for l in range(DEPTH)]).astype(BF16),
        "pool_scale": pool_scale,
        "lbs": lbs,
        "hgrn_norm_g": jnp.tile(hgrn_norm_g, (1, HGRN_WIDTH // HGRN_HEAD_DIM)),
        "ones_bd": _block_diag(head_blocks).astype(BF16),
        "mask_bd": _block_diag(head_blocks),
        "diff_scal": diff_scal, "diff_norm_g": diff_norm_g,
        "ffn_wg": ffn_w_gate.astype(BF16), "ffn_wu": ffn_w_up.astype(BF16), "ffn_wd": ffn_w_down.astype(BF16),
        "router_wt": jnp.swapaxes(router_w, 1, 2), "router_b": router_b[:, :, None],
        "moe_wg": moe_w_gate.astype(BF16), "moe_wu": moe_w_up.astype(BF16), "moe_wd": moe_w_down.astype(BF16),
    }
    y_prompt = _trunk(x_prompt, ada[:, :nb_p], prm)
    y_sample = _trunk(x_sample, ada[:, nb_p:nb_p + nb_s], prm)
    return (y_prompt, y_sample)
```

```python
import functools
import math

import jax
import jax.numpy as jnp
from jax import lax
from jax.experimental import pallas as pl
from jax.experimental.pallas import tpu as pltpu

D_MODEL = 1024
DEPTH = 4
POOL_GROUP_DIM = 64
POOL_WIDTH = 256
POOL_WINDOWS = (2, 4, 8, 16)
POOL_HALO = 16
HGRN_HEAD_DIM = 64
HGRN_WIDTH = 256
DIFF_HEADS = 4
DIFF_HEAD_DIM = 64
DIFF_V_DIM = 128
DIFF_QK_WIDTH = 512
DIFF_WIDTH = 512
D_FF = 2816
N_EXPERTS = 8
RMS_EPS = 1e-6

LANES = 128
VMEM_LIMIT_BYTES = 56 * 1024 * 1024

F32 = jnp.float32
BF16 = jnp.bfloat16


def _params(*semantics):
    return pltpu.CompilerParams(dimension_semantics=semantics, vmem_limit_bytes=VMEM_LIMIT_BYTES)


def _split2(a):
    hi = a.astype(BF16)
    lo = (a - hi.astype(F32)).astype(BF16)
    return hi, lo


def _split3(a):
    hi = a.astype(BF16)
    r = a - hi.astype(F32)
    mid = r.astype(BF16)
    lo = (r - mid.astype(F32)).astype(BF16)
    return hi, mid, lo


def _dot(a, b):
    return jnp.dot(a, b, preferred_element_type=F32)


def _dot_nt(a, b):
    return lax.dot_general(a, b, (((1,), (1,)), ((), ())), preferred_element_type=F32)


def _dot_tn(a, b):
    return lax.dot_general(a, b, (((0,), (0,)), ((), ())), preferred_element_type=F32)


def _dot_f32(a, b):
    ah, al = _split2(a)
    bh, bl = _split2(b)
    return _dot(ah, bh) + _dot(ah, bl) + _dot(al, bh)


def _silu(a):
    return a * jax.nn.sigmoid(a)


def _rms_modulate(x, norm_g, scale, shift):
    y = x * lax.rsqrt(jnp.mean(x * x, axis=-1, keepdims=True) + RMS_EPS) * norm_g
    return y * (1.0 + scale) + shift


def _ada_kernel(c_ref, w_ref, b_ref, o_ref):
    o_ref[0] = _dot_f32(_silu(c_ref[...]), w_ref[0]) + b_ref[0]


def _ada(c_all, ada_w, ada_b):
    rows, d = c_all.shape
    tn = 1024
    n_out = ada_w.shape[-1]
    return pl.pallas_call(
        _ada_kernel,
        grid=(DEPTH, n_out // tn),
        in_specs=[
            pl.BlockSpec((rows, d), lambda l, n: (0, 0)),
            pl.BlockSpec((1, d, tn), lambda l, n: (l, 0, n)),
            pl.BlockSpec((1, 1, tn), lambda l, n: (l, 0, n)),
        ],
        out_specs=pl.BlockSpec((1, rows, tn), lambda l, n: (l, 0, n)),
        out_shape=jax.ShapeDtypeStruct((DEPTH, rows, n_out), F32),
        compiler_params=_params("arbitrary", "arbitrary"),
        name="ada",
    )(c_all, ada_w, ada_b.reshape(DEPTH, 1, n_out))


_PROJ_SPLITS = (
    (0, POOL_WIDTH),
    (POOL_WIDTH, POOL_WIDTH + 5 * HGRN_WIDTH),
    (1536, 2048),
    (2048, 2560),
    (2560, 3072),
)
_PROJ_QK = (2, 3)


def _norm_proj_kernel(*refs, has_f):
    if has_f:
        x_ref, f_ref, g2_ref, ng_ref, sc_ref, sh_ref, w_ref, seg_ref, xo_ref = refs[:9]
        outs = refs[9:]
        x = x_ref[0] + g2_ref[0] * f_ref[0].astype(F32)
        xo_ref[0] = x
    else:
        x_ref, ng_ref, sc_ref, sh_ref, w_ref, seg_ref = refs[:6]
        outs = refs[6:]
        x = x_ref[0]
    hb = _rms_modulate(x, ng_ref[...], sc_ref[0], sh_ref[0]).astype(BF16)
    norms = []
    for k, (o_ref, (a, b)) in enumerate(zip(outs, _PROJ_SPLITS)):
        out = _dot(hb, w_ref[:, a:b]).astype(BF16)
        o_ref[0] = out
        if k in _PROJ_QK:
            sq = out.astype(F32)
            per_row = _dot((sq * sq).astype(BF16), seg_ref[...])
            norms.append(jnp.max(per_row, axis=0, keepdims=True))
    outs[-1][0, 0] = jnp.concatenate(norms, axis=0)


def _norm_proj(x, f, g2, norm_g, sc, sh, w_in, tm=512):
    B, S, D = x.shape
    has_f = f is not None
    row = pl.BlockSpec((1, tm, D), lambda b, i: (b, i, 0))
    per_b = pl.BlockSpec((1, 1, D), lambda b, i: (b, 0, 0))
    in_specs = [row]
    args = [x]
    if has_f:
        in_specs += [row, per_b]
        args += [f, g2]
    seg = (jnp.arange(DIFF_QK_WIDTH)[:, None] // DIFF_HEAD_DIM == jnp.arange(LANES)[None, :]).astype(BF16)
    in_specs += [pl.BlockSpec((1, D), lambda b, i: (0, 0)), per_b, per_b,
                 pl.BlockSpec(w_in.shape, lambda b, i: (0, 0)),
                 pl.BlockSpec(seg.shape, lambda b, i: (0, 0))]
    args += [norm_g.reshape(1, D), sc, sh, w_in, seg]
    out_specs, out_shape = [], []
    if has_f:
        out_specs.append(row)
        out_shape.append(jax.ShapeDtypeStruct((B, S, D), F32))
    for a, b_ in _PROJ_SPLITS:
        out_specs.append(pl.BlockSpec((1, tm, b_ - a), lambda b, i: (b, i, 0)))
        out_shape.append(jax.ShapeDtypeStruct((B, S, b_ - a), BF16))
    out_specs.append(pl.BlockSpec((1, 1, len(_PROJ_QK), LANES), lambda b, i: (b, i, 0, 0)))
    out_shape.append(jax.ShapeDtypeStruct((B, S // tm, len(_PROJ_QK), LANES), F32))
    res = pl.pallas_call(
        functools.partial(_norm_proj_kernel, has_f=has_f),
        grid=(B, S // tm),
        in_specs=in_specs, out_specs=out_specs, out_shape=out_shape,
        compiler_params=_params("parallel", "parallel"),
        name="norm_proj",
    )(*args)
    if has_f:
        return res[0], res[1:]
    return x, res


def _tile_norms(sq_norms, seq_len, ts):
    B, n = sq_norms.shape[:2]
    per_head = sq_norms[..., :2 * DIFF_HEADS].reshape(B, n, len(_PROJ_QK), DIFF_HEADS, 2).max(axis=-1)
    ts = min(ts, seq_len)
    per_tile = per_head.reshape(B, seq_len // ts, n // (seq_len // ts), len(_PROJ_QK), DIFF_HEADS).max(axis=2)
    bound = jnp.sqrt(per_tile).transpose(2, 0, 3, 1)
    return bound[0], bound[1]


def _pool_kernel(prev_ref, u_ref, next_ref, w_ref, scale_ref, o_ref, *, seq_len):
    i = pl.program_id(1)
    n = pl.num_programs(1)
    tm = u_ref.shape[1]
    u = u_ref[0].astype(F32)
    prev = jnp.where(i > 0, prev_ref[0].astype(F32), 0.0)
    nxt = jnp.where(i < n - 1, next_ref[0].astype(F32), 0.0)
    ext = jnp.concatenate([prev, u, nxt], axis=0)
    rows = ext.shape[0]
    group = lax.broadcasted_iota(jnp.int32, (tm, POOL_WIDTH), 1) // POOL_GROUP_DIM
    s = ext
    win = jnp.zeros((tm, POOL_WIDTH), F32)
    for gi, w in enumerate(POOL_WINDOWS):
        s = s + pltpu.roll(s, w // 2, axis=0)
        lead = w // 2 - 1
        centred = s if lead == 0 else pltpu.roll(s, rows - lead, axis=0)
        win = jnp.where(group == gi, centred[POOL_HALO:POOL_HALO + tm], win)
    t = i * tm + lax.broadcasted_iota(jnp.int32, (tm, POOL_WIDTH), 0)
    half = jnp.left_shift(1, group)
    count = jnp.minimum(t + half, seq_len) - jnp.maximum(t - half, 0)
    d = win / count.astype(F32) - u
    o_ref[0] = (_dot(d.astype(BF16), w_ref[...]) * scale_ref[...]).astype(BF16)


def _pool(u, w_bd, scale, tm=512):
    B, S, W = u.shape
    hb = tm // POOL_HALO
    nh = S // POOL_HALO
    return pl.pallas_call(
        functools.partial(_pool_kernel, seq_len=S),
        grid=(B, S // tm),
        in_specs=[
            pl.BlockSpec((1, POOL_HALO, W), lambda b, i: (b, jnp.maximum(i * hb - 1, 0), 0)),
            pl.BlockSpec((1, tm, W), lambda b, i: (b, i, 0)),
            pl.BlockSpec((1, POOL_HALO, W), lambda b, i: (b, jnp.minimum((i + 1) * hb, nh - 1), 0)),
            pl.BlockSpec((W, W), lambda b, i: (0, 0)),
            pl.BlockSpec((1, W), lambda b, i: (0, 0)),
        ],
        out_specs=pl.BlockSpec((1, tm, W), lambda b, i: (b, i, 0)),
        out_shape=jax.ShapeDtypeStruct((B, S, W), BF16),
        compiler_params=_params("parallel", "parallel"),
        name="pool",
    )(u, u, u, w_bd, scale.reshape(1, W))


HGRN_SUB = 16
HGRN_GROUP = 128
HGRN_UNROLL = 8


def _hgrn_kernel(qf_ref, zf_ref, vf_ref, qb_ref, zb_ref, vb_ref, lbf_ref, lbb_ref, ones_ref, mask_ref,
                 of_ref, ob_ref, *scratch):
    c = HGRN_SUB
    tt = qf_ref.shape[1]
    width = qf_ref.shape[2]
    n_sub = tt // c
    ones_bd = ones_ref[...]
    half = len(scratch) // 2
    directions = (
        (False, qf_ref, zf_ref, vf_ref, lbf_ref, of_ref) + tuple(scratch[:half]),
        (True, qb_ref, zb_ref, vb_ref, lbb_ref, ob_ref) + tuple(scratch[half:]),
    )

    @pl.when(pl.program_id(1) == 0)
    def _():
        for d in directions:
            d[6][...] = jnp.zeros_like(d[6])

    for rev, q_ref, z_ref, v_ref, lb_ref, _, _, qe_sc, kt_sc, vb_sc, gam_sc, acc_sc in directions:
        q = _silu(q_ref[0].astype(F32))
        z = z_ref[0].astype(F32)
        v = v_ref[0].astype(F32)
        lb = lb_ref[...]
        f = jnp.maximum(lb + (1.0 - lb) * jax.nn.sigmoid(z), 1e-30)
        g = jnp.log(f)
        kk = (1.0 - lb) * jax.nn.sigmoid(-z)

        r_i = lax.broadcasted_iota(jnp.int32, (HGRN_GROUP, HGRN_GROUP), 0)
        s_i = lax.broadcasted_iota(jnp.int32, (HGRN_GROUP, HGRN_GROUP), 1)
        same = (r_i // c) == (s_i // c)
        tri = (same & ((s_i >= r_i) if rev else (s_i <= r_i))).astype(BF16)
        blk = same.astype(BF16)
        cums, tots = [], []
        for g0 in range(0, tt, HGRN_GROUP):
            parts = _split3(g[g0:g0 + HGRN_GROUP])
            cums.append(sum(_dot(tri, p) for p in parts))
            tots.append(sum(_dot(blk, p) for p in parts))
        cum = jnp.concatenate(cums, axis=0)
        tot = jnp.concatenate(tots, axis=0)

        qe_sc[...] = (q * jnp.exp(cum)).astype(BF16)
        kt_sc[...] = (kk * jnp.exp(tot - cum)).astype(BF16)
        vb_sc[...] = v.astype(BF16)
        gam_sc[...] = jnp.exp(tot)

        pos = lax.broadcasted_iota(jnp.int32, (tt, width), 0) % c
        f_in = jnp.where(pos == (c - 1 if rev else 0), 0.0, f)
        step = tt - 1 if rev else 1
        kd, v_sh = kk, v
        acc = _dot((q * kk).astype(BF16), ones_bd) * v
        for delta in range(1, c):
            kd = pltpu.roll(kd, step, axis=0) * f_in
            v_sh = pltpu.roll(v_sh, step, axis=0)
            acc = acc + _dot((q * kd).astype(BF16), ones_bd) * v_sh
        acc_sc[...] = acc

    mask = mask_ref[...]

    def body(ci, carry):
        for rev, _, _, _, _, _, st_ref, qe_sc, kt_sc, vb_sc, gam_sc, acc_sc in directions:
            idx = (n_sub - 1 - ci) if rev else ci
            r0 = pl.multiple_of(idx * c, c)
            st = st_ref[...]
            acc_sc[pl.ds(r0, c), :] += _dot_nt(qe_sc[pl.ds(r0, c), :], st.astype(BF16))
            kv = _dot_tn(vb_sc[pl.ds(r0, c), :], kt_sc[pl.ds(r0, c), :])
            st_ref[...] = st * gam_sc[pl.ds(r0, 8), :][0:1] + kv * mask
        return carry

    lax.fori_loop(0, n_sub, body, 0, unroll=HGRN_UNROLL)

    for d in directions:
        d[5][0] = d[11][...]


def _hgrn_finish_kernel(of_ref, ob_ref, gate_ref, ng_ref, ones_ref, o_ref):
    o = of_ref[0] + ob_ref[0]
    ms = sum(_dot(p, ones_ref[...]) for p in _split2(o * o)) * (1.0 / HGRN_HEAD_DIM)
    o = o * lax.rsqrt(ms + RMS_EPS) * ng_ref[...] * _silu(gate_ref[0].astype(F32))
    o_ref[0] = o.astype(BF16)


def _hgrn(hg, lb_fwd, lb_bwd, norm_g, ones_bd, mask_bd, tt=256, tm=1024):
    B, S, _ = hg.shape
    W = HGRN_WIDTH
    nt = S // tt

    def col(k, rev):
        return pl.BlockSpec((1, tt, W), lambda b, j: (b, nt - 1 - j if rev else j, k))

    const = lambda shape: pl.BlockSpec(shape, lambda b, j: (0, 0))
    per_direction = [
        pltpu.VMEM((W, W), F32),
        pltpu.VMEM((tt, W), BF16),
        pltpu.VMEM((tt, W), BF16),
        pltpu.VMEM((tt, W), BF16),
        pltpu.VMEM((tt, W), F32),
        pltpu.VMEM((tt, W), F32),
    ]
    o_fwd, o_bwd = pl.pallas_call(
        _hgrn_kernel,
        grid=(B, nt),
        in_specs=[col(0, False), col(1, False), col(3, False), col(0, True), col(2, True), col(3, True),
                  const((1, W)), const((1, W)), const((W, W)), const((W, W))],
        out_specs=[col(0, False), col(0, True)],
        out_shape=[jax.ShapeDtypeStruct((B, S, W), F32)] * 2,
        scratch_shapes=per_direction * 2,
        compiler_params=_params("parallel", "arbitrary"),
        name="hgrn",
    )(hg, hg, hg, hg, hg, hg, lb_fwd.reshape(1, W), lb_bwd.reshape(1, W), ones_bd, mask_bd)

    tm = min(tm, S)
    row = pl.BlockSpec((1, tm, W), lambda b, i: (b, i, 0))
    return pl.pallas_call(
        _hgrn_finish_kernel,
        grid=(B, S // tm),
        in_specs=[row, row, pl.BlockSpec((1, tm, W), lambda b, i: (b, i, 4)),
                  pl.BlockSpec((1, W), lambda b, i: (0, 0)), pl.BlockSpec((W, W), lambda b, i: (0, 0))],
        out_specs=row,
        out_shape=jax.ShapeDtypeStruct((B, S, W), BF16),
        compiler_params=_params("parallel", "parallel"),
        name="hgrn_finish",
    )(o_fwd, o_bwd, hg, norm_g.reshape(1, W), ones_bd)


LOG2E = 1.4426950408889634
ATTN_COL_RADIX = 32
ATTN_SIDES = 3
ATTN_ROW_CHUNK = 256
ATTN_LAG = 1
ATTN_TILE = 1024


def _diff_attn_kernel(pq_ref, pk_ref, cnt_ref, scal_ref, q_ref, k_ref, v_ref, kaug_ref, ng_ref, o_ref,
                      qx_sc, kf_sc, vf_sc, bias_sc, s_sc, pmax_sc, m_sc, acc_sc):
    h = pl.program_id(1)
    t = pl.program_id(2)
    lists = pl.program_id(0) * pl.num_programs(1) + h
    n_pairs = cnt_ref[lists]
    base = lists * (pl.num_programs(2) - ATTN_LAG)
    tq = q_ref.shape[1]
    tk = k_ref.shape[1]
    sigma = scal_ref[2 + h] * LOG2E

    def pair_at(step):
        i = base + jnp.clip(step, 0, n_pairs - 1)
        return pq_ref[i], pk_ref[i]

    qi, ki = pair_at(t)
    qi_done, ki_done = pair_at(t - 1)
    new_query_tile = (t == 0) | (qi != qi_done)
    done_first_of_tile = (t <= 1) | (qi_done != pair_at(t - 2)[0])
    done_last_of_tile = (t >= 1) & (t <= n_pairs) & ((t == n_pairs) | (qi != qi_done))
    chunks = range(0, 2 * tq, ATTN_ROW_CHUNK)

    def sigma_digits(lane_idx, first_lane):
        hi, mid, lo = [p.astype(F32) for p in _split3(jnp.full(lane_idx.shape, sigma, F32))]
        part = (lane_idx - first_lane) % 3
        digits = jnp.where(part == 0, hi, jnp.where(part == 1, mid, lo))
        return jnp.where((lane_idx >= first_lane) & (lane_idx < first_lane + 6), digits, 0.0)

    @pl.when(t == 0)
    def _():
        vcol = lax.broadcasted_iota(jnp.int32, (tk, LANES), 1)
        kf_sc[:, LANES:2 * LANES] = jnp.where(
            vcol < 6, kaug_ref[...].astype(F32), sigma_digits(vcol, 6)).astype(BF16)
        vf_sc[:, LANES:2 * LANES] = jnp.where(vcol == 0, 1.0, 0.0).astype(BF16)
        for r0 in chunks:
            rr = (r0 + lax.broadcasted_iota(jnp.int32, (ATTN_ROW_CHUNK, tk), 0)) % tq
            cc = lax.broadcasted_iota(jnp.int32, (ATTN_ROW_CHUNK, tk), 1)
            bias_sc[r0:r0 + ATTN_ROW_CHUNK] = -sigma * jnp.abs(rr - cc).astype(F32)
        s_sc[1] = jnp.zeros(s_sc.shape[1:], F32)
        pmax_sc[1] = jnp.zeros(pmax_sc.shape[1:], F32)
        m_sc[...] = jnp.zeros(m_sc.shape, F32)
        acc_sc[...] = jnp.zeros(acc_sc.shape, F32)

    @pl.when(new_query_tile)
    def _():
        q = q_ref[0].astype(F32) * (DIFF_HEAD_DIM ** -0.5 * LOG2E)
        lane = lax.broadcasted_iota(jnp.int32, q.shape, 1)
        first = lane < DIFF_HEAD_DIM
        qa = jnp.where(first, q, 0.0).astype(BF16)
        qb = jnp.where(first, 0.0, q).astype(BF16)
        row = lax.broadcasted_iota(jnp.int32, q.shape, 0)
        row_digits = jnp.where(lane < 9, row // ATTN_COL_RADIX * ATTN_COL_RADIX, row % ATTN_COL_RADIX)
        aug = sigma_digits(lane, 0) - jnp.where((lane >= 6) & (lane < 12), row_digits.astype(F32), 0.0)
        for side, a in enumerate((aug, -aug, jnp.zeros_like(aug))):
            qx_sc[side, 0:tq, 0:LANES] = qa
            qx_sc[side, tq:2 * tq, 0:LANES] = qb
            qx_sc[side, 0:tq, LANES:2 * LANES] = a.astype(BF16)
            qx_sc[side, tq:2 * tq, LANES:2 * LANES] = a.astype(BF16)

    n_col = tk // LANES

    kf_sc[:, 0:LANES] = k_ref[0]
    vf_sc[:, 0:LANES] = v_ref[0]
    side = jnp.where(ki < qi, 0, jnp.where(ki > qi, 1, 2))

    def score(overlapping, cur):
        for r0 in chunks:
            rows = slice(r0, r0 + ATTN_ROW_CHUNK)
            s = _dot_nt(qx_sc[side, rows, :], kf_sc[...])
            if overlapping:
                s = s + bias_sc[rows]
            s_sc[cur, rows] = s
            pmax_sc[cur, rows] = functools.reduce(
                jnp.maximum, [s[:, c * LANES:(c + 1) * LANES] for c in range(n_col)])

    def accumulate(old):
        tile_const = -sigma * (jnp.abs(qi_done - ki_done) * tq).astype(F32)
        for r0 in chunks:
            rows = slice(r0, r0 + ATTN_ROW_CHUNK)
            m_old = jnp.where(done_first_of_tile, -jnp.inf, m_sc[rows])
            row_max = jnp.max(pmax_sc[old, rows], axis=-1, keepdims=True) + tile_const
            m_new = jnp.maximum(m_old, row_max)
            alpha = jnp.exp2(m_old - m_new)
            shift = jnp.concatenate([m_new - tile_const] * n_col, axis=1)
            p = jnp.exp2((s_sc[old, rows] - shift).astype(BF16))
            acc_sc[rows] = jnp.concatenate([alpha, alpha], axis=1) * acc_sc[rows] + _dot(p, vf_sc[...])
            m_sc[rows] = m_new

    for parity in (0, 1):
        for overlapping in (False, True):
            @pl.when((t <= n_pairs) & (t % 2 == parity) & ((side == 2) == overlapping))
            def _():
                accumulate(1 - parity)
                score(overlapping, parity)

    @pl.when(done_last_of_tile)
    def _():
        lam = scal_ref[0]
        out_scale = scal_ref[1]
        acc = acc_sc[...]
        o_all = acc[:, 0:DIFF_V_DIM] / acc[:, DIFF_V_DIM:DIFF_V_DIM + 1]
        o = o_all[0:tq] - lam * o_all[tq:2 * tq]
        o = o * lax.rsqrt(jnp.mean(o * o, axis=-1, keepdims=True) + RMS_EPS)
        o_ref[0] = (o * ng_ref[...] * out_scale).astype(BF16)


ATTN_SKIP_MARGIN = 110.0
ATTN_NORM_SLACK = 1.02
ALIBI_SLOPES = tuple(2.0 ** (-8.0 * (h + 1) / DIFF_HEADS) for h in range(DIFF_HEADS))


def _attn_pair_lists(q_norm, k_norm, ts):
    B, H, n = q_norm.shape
    scale = ATTN_NORM_SLACK * DIFF_HEAD_DIM ** -0.5
    i = jnp.arange(n)
    gap = jnp.abs(i[:, None] - i[None, :])
    closest = jnp.where(gap == 0, 0, (gap - 1) * ts + 1).astype(F32)
    slopes = jnp.asarray(ALIBI_SLOPES, F32)[None, :, None, None]
    best = scale * q_norm[:, :, :, None] * k_norm[:, :, None, :] - slopes * closest[None, None]
    floor = -scale * q_norm * k_norm
    active = (best >= floor[:, :, :, None] - ATTN_SKIP_MARGIN) | (gap == 0)[None, None]
    flat = active.reshape(B, H, n * n)
    idx = jnp.arange(n * n, dtype=jnp.int32)
    order = jnp.argsort(jnp.where(flat, idx, idx + n * n), axis=-1).astype(jnp.int32)
    count = jnp.sum(flat, axis=-1).astype(jnp.int32)
    order = jnp.take_along_axis(order, jnp.minimum(idx[None, None], count[..., None] - 1), axis=-1)
    return (order // n).reshape(-1), (order % n).reshape(-1), count.reshape(-1)


def _diff_attn(dq, dk, dv, scal, norm_g, q_norm, k_norm, ts=ATTN_TILE):
    B, S, _ = dq.shape
    ts = min(ts, S)
    assert S % ts == 0 and (2 * ts) % ATTN_ROW_CHUNK == 0 and ts <= ATTN_COL_RADIX * ATTN_COL_RADIX
    nk = S // ts
    n_pairs = nk * nk
    pair_q, pair_k, count = _attn_pair_lists(q_norm, k_norm, ts)
    col = jnp.arange(ts, dtype=jnp.int32)[:, None]
    lane = jnp.arange(LANES, dtype=jnp.int32)[None, :]
    kaug = jnp.where(lane < 3, col // ATTN_COL_RADIX * ATTN_COL_RADIX,
                     jnp.where(lane < 6, col % ATTN_COL_RADIX, 0)).astype(BF16)

    def tile_of(pairs, lag):
        def index_map(b, h, t, pq, pk, cnt):
            lists = b * DIFF_HEADS + h
            return b, pairs(pq, pk)[lists * n_pairs + jnp.clip(t - lag, 0, cnt[lists] - 1)], h
        return index_map

    query_tile = lambda pq, pk: pq
    key_tile = lambda pq, pk: pk
    const = lambda *shape: pl.BlockSpec(shape, lambda b, h, t, pq, pk, cnt: (0,) * len(shape))
    pair_buffer = lambda width, dtype: pltpu.VMEM((2, 2 * ts, width), dtype)
    grid_spec = pltpu.PrefetchScalarGridSpec(
        num_scalar_prefetch=3,
        grid=(B, DIFF_HEADS, n_pairs + ATTN_LAG),
        in_specs=[
            pl.BlockSpec(memory_space=pltpu.SMEM),
            pl.BlockSpec((1, ts, LANES), tile_of(query_tile, 0)),
            pl.BlockSpec((1, ts, LANES), tile_of(key_tile, 0)),
            pl.BlockSpec((1, ts, LANES), tile_of(key_tile, ATTN_LAG)),
            const(ts, LANES),
            const(1, DIFF_V_DIM),
        ],
        out_specs=pl.BlockSpec((1, ts, DIFF_V_DIM), tile_of(query_tile, ATTN_LAG)),
        scratch_shapes=[
            pltpu.VMEM((ATTN_SIDES, 2 * ts, 2 * LANES), BF16),
            pltpu.VMEM((ts, 2 * LANES), BF16),
            pltpu.VMEM((ts, 2 * LANES), BF16),
            pltpu.VMEM((2 * ts, ts), F32),
            pair_buffer(ts, F32),
            pair_buffer(LANES, F32),
            pltpu.VMEM((2 * ts, LANES), F32),
            pltpu.VMEM((2 * ts, 2 * DIFF_V_DIM), F32),
        ],
    )
    return pl.pallas_call(
        _diff_attn_kernel,
        grid_spec=grid_spec,
        out_shape=jax.ShapeDtypeStruct((B, S, DIFF_WIDTH), BF16),
        compiler_params=_params("parallel", "parallel", "arbitrary"),
        name="diff_attn",
    )(pair_q, pair_k, count, scal, dq, dk, dv, kaug, norm_g.reshape(1, DIFF_V_DIM))


def _out_proj_kernel(*refs, has_router):
    (x_ref, pool_ref, hgrn_ref, diff_ref, w_ref, g1_ref, ng_ref, sc_ref, sh_ref) = refs[:9]
    if has_router:
        rw_ref, rb_ref, xo_ref, h2_ref, comb_ref = refs[9:]
    else:
        xo_ref, h2_ref = refs[9:]
    a, b = POOL_WIDTH, POOL_WIDTH + HGRN_WIDTH
    mixed = (_dot(pool_ref[0], w_ref[0:a, :]) + _dot(hgrn_ref[0], w_ref[a:b, :])
             + _dot(diff_ref[0], w_ref[b:, :]))
    x = x_ref[0] + g1_ref[0] * mixed
    xo_ref[0] = x
    h2 = _rms_modulate(x, ng_ref[...], sc_ref[0], sh_ref[0])
    h2_ref[0] = h2.astype(BF16)
    if has_router:
        hh, hl = _split2(h2)
        wh, wl = _split2(rw_ref[...])
        logits = _dot_nt(wh, hh) + _dot_nt(wh, hl) + _dot_nt(wl, hh) + rb_ref[...]
        e_idx = lax.broadcasted_iota(jnp.int32, logits.shape, 0)
        m1 = jnp.max(logits, axis=0, keepdims=True)
        i1 = jnp.min(jnp.where(logits == m1, e_idx, N_EXPERTS), axis=0, keepdims=True)
        first = e_idx == i1
        rest = jnp.where(first, -jnp.inf, logits)
        m2 = jnp.max(rest, axis=0, keepdims=True)
        i2 = jnp.min(jnp.where(rest == m2, e_idx, N_EXPERTS), axis=0, keepdims=True)
        second = e_idx == i2
        e2 = jnp.exp(m2 - m1)
        p1 = 1.0 / (1.0 + e2)
        comb_ref[...] = jnp.where(first, p1, jnp.where(second, e2 * p1, 0.0))


def _out_proj(x, pool_o, hgrn_o, diff_o, w_out, g1, norm_g, sc, sh, router=None, tm=512):
    B, S, D = x.shape
    nt = S // tm
    has_router = router is not None
    row = lambda w: pl.BlockSpec((1, tm, w), lambda b, i: (b, i, 0))
    per_b = pl.BlockSpec((1, 1, D), lambda b, i: (b, 0, 0))
    const = lambda shape: pl.BlockSpec(shape, lambda b, i: (0,) * len(shape))
    in_specs = [row(D), row(POOL_WIDTH), row(HGRN_WIDTH), row(DIFF_WIDTH), const((D, D)),
                per_b, const((1, D)), per_b, per_b]
    args = [x, pool_o, hgrn_o, diff_o, w_out, g1, norm_g.reshape(1, D), sc, sh]
    out_specs = [row(D), row(D)]
    out_shape = [jax.ShapeDtypeStruct((B, S, D), F32), jax.ShapeDtypeStruct((B, S, D), BF16)]
    if has_router:
        rw_t, rb = router
        in_specs += [const((N_EXPERTS, D)), const((N_EXPERTS, 1))]
        args += [rw_t, rb]
        out_specs.append(pl.BlockSpec((N_EXPERTS, tm), lambda b, i: (0, b * nt + i)))
        out_shape.append(jax.ShapeDtypeStruct((N_EXPERTS, B * S), F32))
    return pl.pallas_call(
        functools.partial(_out_proj_kernel, has_router=has_router),
        grid=(B, nt),
        in_specs=in_specs, out_specs=out_specs, out_shape=out_shape,
        compiler_params=_params("parallel", "parallel"),
        name="out_proj",
    )(*args)


def _ffn_kernel(h_ref, wg_ref, wu_ref, wd_ref, o_ref):
    @pl.when(pl.program_id(1) == 0)
    def _():
        o_ref[...] = jnp.zeros_like(o_ref)

    h = h_ref[...]
    mid = _silu(_dot(h, wg_ref[...])) * _dot(h, wu_ref[...])
    o_ref[...] += _dot(mid.astype(BF16), wd_ref[...])


def _ffn(h2, wg, wu, wd, tm=1024, tf=256):
    T, D = h2.shape
    F = wg.shape[-1]
    assert T % tm == 0 and F % tf == 0
    return pl.pallas_call(
        _ffn_kernel,
        grid=(T // tm, F // tf),
        in_specs=[
            pl.BlockSpec((tm, D), lambda i, f: (i, 0)),
            pl.BlockSpec((D, tf), lambda i, f: (0, f)),
            pl.BlockSpec((D, tf), lambda i, f: (0, f)),
            pl.BlockSpec((tf, D), lambda i, f: (f, 0)),
        ],
        out_specs=pl.BlockSpec((tm, D), lambda i, f: (i, 0)),
        out_shape=jax.ShapeDtypeStruct((T, D), F32),
        compiler_params=_params("parallel", "arbitrary"),
        name="ffn",
    )(h2, wg, wu, wd)


MOE_SUB = 256
MOE_TAIL = 128
MOE_RANK_BLOCK = 512


def _moe_kernel(h_ref, comb_ref, wg_ref, wu_ref, wd_ref, o_ref, rank_sc, cnt_sc, xc_sc, yc_sc):
    e = pl.program_id(1)
    fi = pl.program_id(2)
    nf = pl.num_programs(2)
    tm = h_ref.shape[0]

    @pl.when((e == 0) & (fi == 0))
    def _():
        o_ref[...] = jnp.zeros_like(o_ref)
        s_i = lax.broadcasted_iota(jnp.int32, (MOE_RANK_BLOCK, MOE_RANK_BLOCK), 0)
        t_i = lax.broadcasted_iota(jnp.int32, (MOE_RANK_BLOCK, MOE_RANK_BLOCK), 1)
        before = (s_i < t_i).astype(BF16)
        base = jnp.zeros((N_EXPERTS, 1), F32)
        for c0 in range(0, tm, MOE_RANK_BLOCK):
            ind = (comb_ref[:, c0:c0 + MOE_RANK_BLOCK] > 0.0).astype(BF16)
            rank_sc[:, c0:c0 + MOE_RANK_BLOCK] = _dot(ind, before) + base
            base = base + jnp.sum(ind.astype(F32), axis=1, keepdims=True)
        for ex in range(N_EXPERTS):
            cnt_sc[ex] = jnp.sum(base[ex:ex + 1, :]).astype(jnp.int32)

    n_rows = cnt_sc[e]
    n_full = n_rows // MOE_SUB
    rest = n_rows - n_full * MOE_SUB
    n_sub = n_full + (rest > MOE_TAIL).astype(jnp.int32)
    has_tail = (rest > 0) & (rest <= MOE_TAIL)
    sel = comb_ref[pl.ds(e, 1), :]
    rank = rank_sc[pl.ds(e, 1), :]

    def for_blocks(fn):
        def body(sub, carry):
            fn(pl.multiple_of(sub * MOE_SUB, MOE_SUB), MOE_SUB)
            return carry
        lax.fori_loop(0, n_sub, body, 0)

        @pl.when(has_tail)
        def _():
            fn(pl.multiple_of(n_full * MOE_SUB, MOE_SUB), MOE_TAIL)

    def onehot(r0, size):
        slot = lax.broadcasted_iota(jnp.int32, (size, tm), 0) + r0
        return (rank == slot.astype(F32)) & (sel > 0.0)

    @pl.when(fi == 0)
    def _():
        def gather(r0, size):
            g = onehot(r0, size).astype(BF16)
            xc_sc[pl.ds(r0, size), :] = _dot(g, h_ref[...]).astype(BF16)
            yc_sc[pl.ds(r0, size), :] = jnp.zeros((size, yc_sc.shape[1]), F32)
        for_blocks(gather)

    def expert(r0, size):
        xs = xc_sc[pl.ds(r0, size), :]
        mid = _silu(_dot(xs, wg_ref[0])) * _dot(xs, wu_ref[0])
        yc_sc[pl.ds(r0, size), :] += _dot(mid.astype(BF16), wd_ref[0])
    for_blocks(expert)

    @pl.when(fi == nf - 1)
    def _():
        def scatter(r0, size):
            hit = onehot(r0, size)
            weight = jnp.sum(jnp.where(hit, sel, 0.0), axis=1, keepdims=True)
            ys = (yc_sc[pl.ds(r0, size), :] * weight).astype(BF16)
            o_ref[...] = (o_ref[...].astype(F32) + _dot_tn(hit.astype(BF16), ys)).astype(o_ref.dtype)
        for_blocks(scatter)


def _moe(h2, comb, wg, wu, wd, tm=2048, tf=1408):
    T, D = h2.shape
    F = wg.shape[-1]
    assert T % tm == 0 and F % tf == 0
    return pl.pallas_call(
        _moe_kernel,
        grid=(T // tm, N_EXPERTS, F // tf),
        in_specs=[
            pl.BlockSpec((tm, D), lambda i, e, f: (i, 0)),
            pl.BlockSpec((N_EXPERTS, tm), lambda i, e, f: (0, i)),
            pl.BlockSpec((1, D, tf), lambda i, e, f: (e, 0, f)),
            pl.BlockSpec((1, D, tf), lambda i, e, f: (e, 0, f)),
            pl.BlockSpec((1, tf, D), lambda i, e, f: (e, f, 0)),
        ],
        out_specs=pl.BlockSpec((tm, D), lambda i, e, f: (i, 0)),
        out_shape=jax.ShapeDtypeStruct((T, D), BF16),
        scratch_shapes=[
            pltpu.VMEM((N_EXPERTS, tm), F32),
            pltpu.SMEM((N_EXPERTS,), jnp.int32),
            pltpu.VMEM((tm, D), BF16),
            pltpu.VMEM((tm, D), F32),
        ],
        compiler_params=_params("parallel", "arbitrary", "arbitrary"),
        name="moe",
    )(h2, comb, wg, wu, wd)


def _final_kernel(x_ref, f_ref, g2_ref, ng_ref, o_ref):
    x = x_ref[0] + g2_ref[0] * f_ref[0].astype(F32)
    o_ref[0] = x * lax.rsqrt(jnp.mean(x * x, axis=-1, keepdims=True) + RMS_EPS) * ng_ref[...]


def _final(x, f, g2, norm_g, tm=1024):
    B, S, D = x.shape
    row = pl.BlockSpec((1, tm, D), lambda b, i: (b, i, 0))
    return pl.pallas_call(
        _final_kernel,
        grid=(B, S // tm),
        in_specs=[row, row, pl.BlockSpec((1, 1, D), lambda b, i: (b, 0, 0)),
                  pl.BlockSpec((1, D), lambda b, i: (0, 0))],
        out_specs=row,
        out_shape=jax.ShapeDtypeStruct((B, S, D), F32),
        compiler_params=_params("parallel", "parallel"),
        name="final_norm",
    )(x, f, g2, norm_g.reshape(1, D))


def _block_diag(blocks):
    n, r, c = blocks.shape
    out = jnp.zeros((n * r, n * c), blocks.dtype)
    for i in range(n):
        out = out.at[i * r:(i + 1) * r, i * c:(i + 1) * c].set(blocks[i])
    return out


def _trunk(x, ada, prm):
    B, S, D = x.shape
    f = None
    g2 = None
    for l in range(DEPTH):
        sh1, sc1, g1, sh2, sc2, g2_l = [ada[l][:, None, k * D:(k + 1) * D] for k in range(6)]
        x, (u, hg, dq, dk, dv, sq_norms) = _norm_proj(x, f, g2, prm["norm1_g"][l], sc1, sh1, prm["w_in"][l])
        q_norm, k_norm = _tile_norms(sq_norms, S, ATTN_TILE)
        pool_o = _pool(u, prm["pool_bd"][l], prm["pool_scale"][l])
        hgrn_o = _hgrn(hg, prm["lbs"][l, 0], prm["lbs"][l, 1], prm["hgrn_norm_g"][l],
                       prm["ones_bd"], prm["mask_bd"])
        diff_o = _diff_attn(dq, dk, dv, prm["diff_scal"][l], prm["diff_norm_g"][l], q_norm, k_norm)
        moe = l % 2 == 1
        j = l // 2
        router = (prm["router_wt"][j], prm["router_b"][j]) if moe else None
        res = _out_proj(x, pool_o, hgrn_o, diff_o, prm["w_out"][l], g1, prm["norm2_g"][l], sc2, sh2,
                        router=router)
        x = res[0]
        h2 = res[1].reshape(B * S, D)
        if moe:
            f = _moe(h2, res[2], prm["moe_wg"][j], prm["moe_wu"][j], prm["moe_wd"][j])
        else:
            f = _ffn(h2, prm["ffn_wg"][j], prm["ffn_wu"][j], prm["ffn_wd"][j])
        f = f.reshape(B, S, D)
        g2 = g2_l
    return _final(x, f, g2, prm["final_norm_g"])


def kernel(x_prompt, x_sample, c_prompt, c_sample, ada_w, ada_b, norm1_g, norm2_g, w_in, pool_w, pool_scale, hgrn_lb, hgrn_norm_g, diff_lambda, diff_norm_g, w_out, ffn_w_gate, ffn_w_up, ffn_w_down, router_w, router_b, moe_w_gate, moe_w_up, moe_w_down, final_norm_g):
    nb_p, nb_s = c_prompt.shape[0], c_sample.shape[0]
    rows = -(-(nb_p + nb_s) // 8) * 8
    c_all = jnp.zeros((rows, D_MODEL), F32).at[:nb_p].set(c_prompt).at[nb_p:nb_p + nb_s].set(c_sample)
    ada = _ada(c_all, ada_w, ada_b)

    p_lb = jax.nn.softmax(hgrn_lb.astype(F32), axis=0)
    lbs = jnp.cumsum(p_lb, axis=0) - p_lb[0:1]
    lv = diff_lambda.astype(F32)
    lam_init = jnp.asarray([0.8 - 0.6 * math.exp(-0.3 * l) for l in range(DEPTH)], F32)
    lam = jnp.exp(jnp.sum(lv[:, 0] * lv[:, 1], axis=-1)) - jnp.exp(jnp.sum(lv[:, 2] * lv[:, 3], axis=-1)) + lam_init
    slopes = jnp.asarray([2.0 ** (-8.0 * (h + 1) / DIFF_HEADS) for h in range(DIFF_HEADS)], F32)
    diff_scal = jnp.concatenate(
        [lam[:, None], (1.0 - lam_init)[:, None], jnp.broadcast_to(slopes, (DEPTH, DIFF_HEADS)),
         jnp.zeros((DEPTH, 2), F32)], axis=1)
    head_blocks = jnp.ones((HGRN_WIDTH // HGRN_HEAD_DIM, HGRN_HEAD_DIM, HGRN_HEAD_DIM), F32)
    prm = {
        "norm1_g": norm1_g, "norm2_g": norm2_g, "final_norm_g": final_norm_g,
        "w_in": w_in.astype(BF16), "w_out": w_out.astype(BF16),
        "pool_bd": jnp.stack([_block_diag(pool_w[l]) for l in range(DEPTH)]).astype(BF16),
        "pool_scale": pool_scale,
        "lbs": lbs,
        "hgrn_norm_g": jnp.tile(hgrn_norm_g, (1, HGRN_WIDTH // HGRN_HEAD_DIM)),
        "ones_bd": _block_diag(head_blocks).astype(BF16),
        "mask_bd": _block_diag(head_blocks),
        "diff_scal": diff_scal, "diff_norm_g": diff_norm_g,
        "ffn_wg": ffn_w_gate.astype(BF16), "ffn_wu": ffn_w_up.astype(BF16), "ffn_wd": ffn_w_down.astype(BF16),
        "router_wt": jnp.swapaxes(router_w, 1, 2), "router_b": router_b[:, :, None],
        "moe_wg": moe_w_gate.astype(BF16), "moe_wu": moe_w_up.astype(BF16), "moe_wd": moe_w_down.astype(BF16),
    }
    y_prompt = _trunk(x_prompt, ada[:, :nb_p], prm)
    y_sample = _trunk(x_sample, ada[:, nb_p:nb_p + nb_s], prm)
    return (y_prompt, y_sample)
```

```python
import functools
import math

import jax
import jax.numpy as jnp
from jax import lax
from jax.experimental import pallas as pl
from jax.experimental.pallas import tpu as pltpu

D_MODEL = 1024
DEPTH = 4
POOL_GROUP_DIM = 64
POOL_WIDTH = 256
POOL_WINDOWS = (2, 4, 8, 16)
POOL_HALO = 16
HGRN_HEAD_DIM = 64
HGRN_WIDTH = 256
DIFF_HEADS = 4
DIFF_HEAD_DIM = 64
DIFF_V_DIM = 128
DIFF_QK_WIDTH = 512
DIFF_WIDTH = 512
D_FF = 2816
N_EXPERTS = 8
RMS_EPS = 1e-6

LANES = 128
VMEM_LIMIT_BYTES = 56 * 1024 * 1024

F32 = jnp.float32
BF16 = jnp.bfloat16


def _params(*semantics):
    return pltpu.CompilerParams(dimension_semantics=semantics, vmem_limit_bytes=VMEM_LIMIT_BYTES)


def _split2(a):
    hi = a.astype(BF16)
    lo = (a - hi.astype(F32)).astype(BF16)
    return hi, lo


def _split3(a):
    hi = a.astype(BF16)
    r = a - hi.astype(F32)
    mid = r.astype(BF16)
    lo = (r - mid.astype(F32)).astype(BF16)
    return hi, mid, lo


def _dot(a, b):
    return jnp.dot(a, b, preferred_element_type=F32)


def _dot_nt(a, b):
    return lax.dot_general(a, b, (((1,), (1,)), ((), ())), preferred_element_type=F32)


def _dot_tn(a, b):
    return lax.dot_general(a, b, (((0,), (0,)), ((), ())), preferred_element_type=F32)


def _dot_f32(a, b):
    ah, al = _split2(a)
    bh, bl = _split2(b)
    return _dot(ah, bh) + _dot(ah, bl) + _dot(al, bh)


def _silu(a):
    return a * jax.nn.sigmoid(a)


def _rms_modulate(x, norm_g, scale, shift):
    y = x * lax.rsqrt(jnp.mean(x * x, axis=-1, keepdims=True) + RMS_EPS) * norm_g
    return y * (1.0 + scale) + shift


def _ada_kernel(c_ref, w_ref, b_ref, o_ref):
    o_ref[0] = _dot_f32(_silu(c_ref[...]), w_ref[0]) + b_ref[0]


def _ada(c_all, ada_w, ada_b):
    rows, d = c_all.shape
    tn = 1024
    n_out = ada_w.shape[-1]
    return pl.pallas_call(
        _ada_kernel,
        grid=(DEPTH, n_out // tn),
        in_specs=[
            pl.BlockSpec((rows, d), lambda l, n: (0, 0)),
            pl.BlockSpec((1, d, tn), lambda l, n: (l, 0, n)),
            pl.BlockSpec((1, 1, tn), lambda l, n: (l, 0, n)),
        ],
        out_specs=pl.BlockSpec((1, rows, tn), lambda l, n: (l, 0, n)),
        out_shape=jax.ShapeDtypeStruct((DEPTH, rows, n_out), F32),
        compiler_params=_params("arbitrary", "arbitrary"),
        name="ada",
    )(c_all, ada_w, ada_b.reshape(DEPTH, 1, n_out))


_PROJ_SPLITS = (
    (0, POOL_WIDTH),
    (POOL_WIDTH, POOL_WIDTH + 5 * HGRN_WIDTH),
    (1536, 2048),
    (2048, 2560),
    (2560, 3072),
)
_PROJ_QK = (2, 3)


def _norm_proj_kernel(*refs, has_f):
    if has_f:
        x_ref, f_ref, g2_ref, ng_ref, sc_ref, sh_ref, w_ref, seg_ref, xo_ref = refs[:9]
        outs = refs[9:]
        x = x_ref[0] + g2_ref[0] * f_ref[0].astype(F32)
        xo_ref[0] = x
    else:
        x_ref, ng_ref, sc_ref, sh_ref, w_ref, seg_ref = refs[:6]
        outs = refs[6:]
        x = x_ref[0]
    hb = _rms_modulate(x, ng_ref[...], sc_ref[0], sh_ref[0]).astype(BF16)
    norms = []
    for k, (o_ref, (a, b)) in enumerate(zip(outs, _PROJ_SPLITS)):
        out = _dot(hb, w_ref[:, a:b]).astype(BF16)
        o_ref[0] = out
        if k in _PROJ_QK:
            sq = out.astype(F32)
            per_row = _dot((sq * sq).astype(BF16), seg_ref[...])
            norms.append(jnp.max(per_row, axis=0, keepdims=True))
    outs[-1][0, 0] = jnp.concatenate(norms, axis=0)


def _norm_proj(x, f, g2, norm_g, sc, sh, w_in, tm=512):
    B, S, D = x.shape
    has_f = f is not None
    row = pl.BlockSpec((1, tm, D), lambda b, i: (b, i, 0))
    per_b = pl.BlockSpec((1, 1, D), lambda b, i: (b, 0, 0))
    in_specs = [row]
    args = [x]
    if has_f:
        in_specs += [row, per_b]
        args += [f, g2]
    seg = (jnp.arange(DIFF_QK_WIDTH)[:, None] // DIFF_HEAD_DIM == jnp.arange(LANES)[None, :]).astype(BF16)
    in_specs += [pl.BlockSpec((1, D), lambda b, i: (0, 0)), per_b, per_b,
                 pl.BlockSpec(w_in.shape, lambda b, i: (0, 0)),
                 pl.BlockSpec(seg.shape, lambda b, i: (0, 0))]
    args += [norm_g.reshape(1, D), sc, sh, w_in, seg]
    out_specs, out_shape = [], []
    if has_f:
        out_specs.append(row)
        out_shape.append(jax.ShapeDtypeStruct((B, S, D), F32))
    for a, b_ in _PROJ_SPLITS:
        out_specs.append(pl.BlockSpec((1, tm, b_ - a), lambda b, i: (b, i, 0)))
        out_shape.append(jax.ShapeDtypeStruct((B, S, b_ - a), BF16))
    out_specs.append(pl.BlockSpec((1, 1, len(_PROJ_QK), LANES), lambda b, i: (b, i, 0, 0)))
    out_shape.append(jax.ShapeDtypeStruct((B, S // tm, len(_PROJ_QK), LANES), F32))
    res = pl.pallas_call(
        functools.partial(_norm_proj_kernel, has_f=has_f),
        grid=(B, S // tm),
        in_specs=in_specs, out_specs=out_specs, out_shape=out_shape,
        compiler_params=_params("parallel", "parallel"),
        name="norm_proj",
    )(*args)
    if has_f:
        return res[0], res[1:]
    return x, res


def _tile_norms(sq_norms, seq_len, ts):
    B, n = sq_norms.shape[:2]
    per_head = sq_norms[..., :2 * DIFF_HEADS].reshape(B, n, len(_PROJ_QK), DIFF_HEADS, 2).max(axis=-1)
    ts = min(ts, seq_len)
    per_tile = per_head.reshape(B, seq_len // ts, n // (seq_len // ts), len(_PROJ_QK), DIFF_HEADS).max(axis=2)
    bound = jnp.sqrt(per_tile).transpose(2, 0, 3, 1)
    return bound[0], bound[1]


def _pool_kernel(prev_ref, u_ref, next_ref, w_ref, scale_ref, o_ref, *, seq_len):
    i = pl.program_id(1)
    n = pl.num_programs(1)
    tm = u_ref.shape[1]
    u = u_ref[0].astype(F32)
    prev = jnp.where(i > 0, prev_ref[0].astype(F32), 0.0)
    nxt = jnp.where(i < n - 1, next_ref[0].astype(F32), 0.0)
    ext = jnp.concatenate([prev, u, nxt], axis=0)
    rows = ext.shape[0]
    group = lax.broadcasted_iota(jnp.int32, (tm, POOL_WIDTH), 1) // POOL_GROUP_DIM
    s = ext
    win = jnp.zeros((tm, POOL_WIDTH), F32)
    for gi, w in enumerate(POOL_WINDOWS):
        s = s + pltpu.roll(s, w // 2, axis=0)
        lead = w // 2 - 1
        centred = s if lead == 0 else pltpu.roll(s, rows - lead, axis=0)
        win = jnp.where(group == gi, centred[POOL_HALO:POOL_HALO + tm], win)
    t = i * tm + lax.broadcasted_iota(jnp.int32, (tm, POOL_WIDTH), 0)
    half = jnp.left_shift(1, group)
    count = jnp.minimum(t + half, seq_len) - jnp.maximum(t - half, 0)
    d = win / count.astype(F32) - u
    o_ref[0] = (_dot(d.astype(BF16), w_ref[...]) * scale_ref[...]).astype(BF16)


def _pool(u, w_bd, scale, tm=512):
    B, S, W = u.shape
    hb = tm // POOL_HALO
    nh = S // POOL_HALO
    return pl.pallas_call(
        functools.partial(_pool_kernel, seq_len=S),
        grid=(B, S // tm),
        in_specs=[
            pl.BlockSpec((1, POOL_HALO, W), lambda b, i: (b, jnp.maximum(i * hb - 1, 0), 0)),
            pl.BlockSpec((1, tm, W), lambda b, i: (b, i, 0)),
            pl.BlockSpec((1, POOL_HALO, W), lambda b, i: (b, jnp.minimum((i + 1) * hb, nh - 1), 0)),
            pl.BlockSpec((W, W), lambda b, i: (0, 0)),
            pl.BlockSpec((1, W), lambda b, i: (0, 0)),
        ],
        out_specs=pl.BlockSpec((1, tm, W), lambda b, i: (b, i, 0)),
        out_shape=jax.ShapeDtypeStruct((B, S, W), BF16),
        compiler_params=_params("parallel", "parallel"),
        name="pool",
    )(u, u, u, w_bd, scale.reshape(1, W))


HGRN_SUB = 16
HGRN_UNROLL = 16


def _hgrn_kernel(qf_ref, zf_ref, vf_ref, qb_ref, zb_ref, vb_ref, lbf_ref, lbb_ref, ones_ref, mask_ref,
                 perm_ref, unperm_ref, of_ref, ob_ref, *scratch):
    c = HGRN_SUB
    tt = qf_ref.shape[1]
    n_sub = tt // c
    ones_bd = ones_ref[...]
    half = len(scratch) // 2
    directions = (
        (False, qf_ref, zf_ref, vf_ref, lbf_ref, of_ref) + tuple(scratch[:half]),
        (True, qb_ref, zb_ref, vb_ref, lbb_ref, ob_ref) + tuple(scratch[half:]),
    )

    @pl.when(pl.program_id(1) == 0)
    def _():
        for d in directions:
            d[6][...] = jnp.zeros_like(d[6])

    r_i = lax.broadcasted_iota(jnp.int32, (tt, tt), 0)
    s_i = lax.broadcasted_iota(jnp.int32, (tt, tt), 1)
    same_sub = (r_i % n_sub) == (s_i % n_sub)
    blk = same_sub.astype(BF16)

    for rev, q_ref, z_ref, v_ref, lb_ref, _, _, qe_sc, kt_sc, vb_sc, gam_sc, acc_sc in directions:
        perm = perm_ref[...]
        q = _silu(_dot(perm, q_ref[0]))
        z = _dot(perm, z_ref[0])
        v = _dot(perm, v_ref[0])
        lb = lb_ref[...]
        f = jnp.maximum(lb + (1.0 - lb) * jax.nn.sigmoid(z), 1e-30)
        g = jnp.log(f)
        kk = (1.0 - lb) * jax.nn.sigmoid(-z)

        tri = (same_sub & ((s_i // n_sub >= r_i // n_sub) if rev
                           else (s_i // n_sub <= r_i // n_sub))).astype(BF16)
        parts = _split3(g)
        cum = sum(_dot(tri, p) for p in parts)
        tot = sum(_dot(blk, p) for p in parts)

        unperm = unperm_ref[...]
        qe_sc[...] = _dot(unperm, (q * jnp.exp(cum)).astype(BF16)).astype(BF16)
        kt_sc[...] = _dot(unperm, (kk * jnp.exp(tot - cum)).astype(BF16)).astype(BF16)
        vb_sc[...] = v_ref[0]
        gam_sc[...] = jnp.exp(tot[0:n_sub])

        kd = kk
        acc_sc[...] = _dot((q * kk).astype(BF16), ones_bd) * v
        for delta in range(1, c):
            n = tt - n_sub * delta
            tgt = slice(0, n) if rev else slice(n_sub * delta, tt)
            src = slice(n_sub * delta, tt) if rev else slice(0, n)
            kd = (kd[n_sub:] if rev else kd[:n]) * f[tgt]
            acc_sc[tgt] += _dot((q[tgt] * kd).astype(BF16), ones_bd) * v[src]
        acc_sc[...] = sum(_dot(unperm, p) for p in _split3(acc_sc[...]))

    mask = mask_ref[0:LANES, 0:LANES]

    def body(ci, carry):
        for rev, _, _, _, _, _, st_ref, qe_sc, kt_sc, vb_sc, gam_sc, acc_sc in directions:
            idx = (n_sub - 1 - ci) if rev else ci
            r0 = pl.multiple_of(idx * c, c)
            gam = gam_sc[pl.ds(idx, 1), :]
            for lanes in (slice(0, LANES), slice(LANES, 2 * LANES)):
                st = st_ref[lanes, lanes]
                acc_sc[pl.ds(r0, c), lanes] += _dot_nt(qe_sc[pl.ds(r0, c), lanes], st.astype(BF16))
                kv = _dot_tn(vb_sc[pl.ds(r0, c), lanes], kt_sc[pl.ds(r0, c), lanes])
                st_ref[lanes, lanes] = st * gam[:, lanes] + kv * mask
        return carry

    lax.fori_loop(0, n_sub, body, 0, unroll=HGRN_UNROLL)

    for d in directions:
        d[5][0] = d[11][...]


def _hgrn_finish_kernel(of_ref, ob_ref, gate_ref, ng_ref, ones_ref, o_ref):
    o = of_ref[0] + ob_ref[0]
    ms = sum(_dot(p, ones_ref[...]) for p in _split2(o * o)) * (1.0 / HGRN_HEAD_DIM)
    o = o * lax.rsqrt(ms + RMS_EPS) * ng_ref[...] * _silu(gate_ref[0].astype(F32))
    o_ref[0] = o.astype(BF16)


def _hgrn(hg, lb_fwd, lb_bwd, norm_g, ones_bd, mask_bd, tt=256, tm=1024):
    B, S, _ = hg.shape
    W = HGRN_WIDTH
    nt = S // tt

    def col(k, rev):
        return pl.BlockSpec((1, tt, W), lambda b, j: (b, nt - 1 - j if rev else j, k))

    const = lambda shape: pl.BlockSpec(shape, lambda b, j: (0, 0))
    per_direction = [
        pltpu.VMEM((W, W), F32),
        pltpu.VMEM((tt, W), BF16),
        pltpu.VMEM((tt, W), BF16),
        pltpu.VMEM((tt, W), BF16),
        pltpu.VMEM((tt // HGRN_SUB, W), F32),
        pltpu.VMEM((tt, W), F32),
    ]
    token = jnp.arange(tt)
    perm = (token[None, :] == (token % (tt // HGRN_SUB) * HGRN_SUB + token // (tt // HGRN_SUB))[:, None])
    perm = perm.astype(BF16)
    o_fwd, o_bwd = pl.pallas_call(
        _hgrn_kernel,
        grid=(B, nt),
        in_specs=[col(0, False), col(1, False), col(3, False), col(0, True), col(2, True), col(3, True),
                  const((1, W)), const((1, W)), const((W, W)), const((W, W)),
                  const((tt, tt)), const((tt, tt))],
        out_specs=[col(0, False), col(0, True)],
        out_shape=[jax.ShapeDtypeStruct((B, S, W), F32)] * 2,
        scratch_shapes=per_direction * 2,
        compiler_params=_params("parallel", "arbitrary"),
        name="hgrn",
    )(hg, hg, hg, hg, hg, hg, lb_fwd.reshape(1, W), lb_bwd.reshape(1, W), ones_bd, mask_bd, perm, perm.T)

    tm = min(tm, S)
    row = pl.BlockSpec((1, tm, W), lambda b, i: (b, i, 0))
    return pl.pallas_call(
        _hgrn_finish_kernel,
        grid=(B, S // tm),
        in_specs=[row, row, pl.BlockSpec((1, tm, W), lambda b, i: (b, i, 4)),
                  pl.BlockSpec((1, W), lambda b, i: (0, 0)), pl.BlockSpec((W, W), lambda b, i: (0, 0))],
        out_specs=row,
        out_shape=jax.ShapeDtypeStruct((B, S, W), BF16),
        compiler_params=_params("parallel", "parallel"),
        name="hgrn_finish",
    )(o_fwd, o_bwd, hg, norm_g.reshape(1, W), ones_bd)


LOG2E = 1.4426950408889634
ATTN_COL_RADIX = 32
ATTN_SIDES = 3
ATTN_ROW_CHUNK = 256
ATTN_LAG = 1
ATTN_TILE = 1024


def _diff_attn_kernel(pq_ref, pk_ref, cnt_ref, scal_ref, q_ref, k_ref, v_ref, kaug_ref, ng_ref, o_ref,
                      qx_sc, kf_sc, vf_sc, bias_sc, s_sc, pmax_sc, m_sc, acc_sc):
    h = pl.program_id(1)
    t = pl.program_id(2)
    lists = pl.program_id(0) * pl.num_programs(1) + h
    n_pairs = cnt_ref[lists]
    base = lists * (pl.num_programs(2) - ATTN_LAG)
    tq = q_ref.shape[1]
    tk = k_ref.shape[1]
    sigma = scal_ref[2 + h] * LOG2E

    def pair_at(step):
        i = base + jnp.clip(step, 0, n_pairs - 1)
        return pq_ref[i], pk_ref[i]

    qi, ki = pair_at(t)
    qi_done, ki_done = pair_at(t - 1)
    new_query_tile = (t == 0) | (qi != qi_done)
    done_first_of_tile = (t <= 1) | (qi_done != pair_at(t - 2)[0])
    done_last_of_tile = (t >= 1) & (t <= n_pairs) & ((t == n_pairs) | (qi != qi_done))
    chunks = range(0, 2 * tq, ATTN_ROW_CHUNK)

    def sigma_digits(lane_idx, first_lane):
        hi, mid, lo = [p.astype(F32) for p in _split3(jnp.full(lane_idx.shape, sigma, F32))]
        part = (lane_idx - first_lane) % 3
        digits = jnp.where(part == 0, hi, jnp.where(part == 1, mid, lo))
        return jnp.where((lane_idx >= first_lane) & (lane_idx < first_lane + 6), digits, 0.0)

    @pl.when(t == 0)
    def _():
        vcol = lax.broadcasted_iota(jnp.int32, (tk, LANES), 1)
        kf_sc[:, LANES:2 * LANES] = jnp.where(
            vcol < 6, kaug_ref[...].astype(F32), sigma_digits(vcol, 6)).astype(BF16)
        vf_sc[:, LANES:2 * LANES] = jnp.where(vcol == 0, 1.0, 0.0).astype(BF16)
        for r0 in chunks:
            rr = (r0 + lax.broadcasted_iota(jnp.int32, (ATTN_ROW_CHUNK, tk), 0)) % tq
            cc = lax.broadcasted_iota(jnp.int32, (ATTN_ROW_CHUNK, tk), 1)
            bias_sc[r0:r0 + ATTN_ROW_CHUNK] = -sigma * jnp.abs(rr - cc).astype(F32)
        s_sc[1] = jnp.zeros(s_sc.shape[1:], F32)
        pmax_sc[1] = jnp.zeros(pmax_sc.shape[1:], F32)
        m_sc[...] = jnp.zeros(m_sc.shape, F32)
        acc_sc[...] = jnp.zeros(acc_sc.shape, F32)

    @pl.when(new_query_tile)
    def _():
        q = q_ref[0].astype(F32) * (DIFF_HEAD_DIM ** -0.5 * LOG2E)
        lane = lax.broadcasted_iota(jnp.int32, q.shape, 1)
        first = lane < DIFF_HEAD_DIM
        qa = jnp.where(first, q, 0.0).astype(BF16)
        qb = jnp.where(first, 0.0, q).astype(BF16)
        row = lax.broadcasted_iota(jnp.int32, q.shape, 0)
        row_digits = jnp.where(lane < 9, row // ATTN_COL_RADIX * ATTN_COL_RADIX, row % ATTN_COL_RADIX)
        aug = sigma_digits(lane, 0) - jnp.where((lane >= 6) & (lane < 12), row_digits.astype(F32), 0.0)
        for side, a in enumerate((aug, -aug, jnp.zeros_like(aug))):
            qx_sc[side, 0:tq, 0:LANES] = qa
            qx_sc[side, tq:2 * tq, 0:LANES] = qb
            qx_sc[side, 0:tq, LANES:2 * LANES] = a.astype(BF16)
            qx_sc[side, tq:2 * tq, LANES:2 * LANES] = a.astype(BF16)

    n_col = tk // LANES

    kf_sc[:, 0:LANES] = k_ref[0]
    vf_sc[:, 0:LANES] = v_ref[0]
    side = jnp.where(ki < qi, 0, jnp.where(ki > qi, 1, 2))

    def score(overlapping, cur):
        for r0 in chunks:
            rows = slice(r0, r0 + ATTN_ROW_CHUNK)
            s = _dot_nt(qx_sc[side, rows, :], kf_sc[...])
            if overlapping:
                s = s + bias_sc[rows]
            s_sc[cur, rows] = s
            pmax_sc[cur, rows] = functools.reduce(
                jnp.maximum, [s[:, c * LANES:(c + 1) * LANES] for c in range(n_col)])

    def accumulate(old):
        tile_const = -sigma * (jnp.abs(qi_done - ki_done) * tq).astype(F32)
        for r0 in chunks:
            rows = slice(r0, r0 + ATTN_ROW_CHUNK)
            m_old = jnp.where(done_first_of_tile, -jnp.inf, m_sc[rows])
            row_max = jnp.max(pmax_sc[old, rows], axis=-1, keepdims=True) + tile_const
            m_new = jnp.maximum(m_old, row_max)
            alpha = jnp.exp2(m_old - m_new)
            shift = jnp.concatenate([m_new - tile_const] * n_col, axis=1)
            p = jnp.exp2((s_sc[old, rows] - shift).astype(BF16))
            acc_sc[rows] = jnp.concatenate([alpha, alpha], axis=1) * acc_sc[rows] + _dot(p, vf_sc[...])
            m_sc[rows] = m_new

    for parity in (0, 1):
        for overlapping in (False, True):
            @pl.when((t <= n_pairs) & (t % 2 == parity) & ((side == 2) == overlapping))
            def _():
                accumulate(1 - parity)
                score(overlapping, parity)

    @pl.when(done_last_of_tile)
    def _():
        lam = scal_ref[0]
        out_scale = scal_ref[1]
        acc = acc_sc[...]
        o_all = acc[:, 0:DIFF_V_DIM] / acc[:, DIFF_V_DIM:DIFF_V_DIM + 1]
        o = o_all[0:tq] - lam * o_all[tq:2 * tq]
        o = o * lax.rsqrt(jnp.mean(o * o, axis=-1, keepdims=True) + RMS_EPS)
        o_ref[0] = (o * ng_ref[...] * out_scale).astype(BF16)


ATTN_SKIP_MARGIN = 110.0
ATTN_NORM_SLACK = 1.02
ALIBI_SLOPES = tuple(2.0 ** (-8.0 * (h + 1) / DIFF_HEADS) for h in range(DIFF_HEADS))


def _attn_pair_lists(q_norm, k_norm, ts):
    B, H, n = q_norm.shape
    scale = ATTN_NORM_SLACK * DIFF_HEAD_DIM ** -0.5
    i = jnp.arange(n)
    gap = jnp.abs(i[:, None] - i[None, :])
    closest = jnp.where(gap == 0, 0, (gap - 1) * ts + 1).astype(F32)
    slopes = jnp.asarray(ALIBI_SLOPES, F32)[None, :, None, None]
    best = scale * q_norm[:, :, :, None] * k_norm[:, :, None, :] - slopes * closest[None, None]
    floor = -scale * q_norm * k_norm
    active = (best >= floor[:, :, :, None] - ATTN_SKIP_MARGIN) | (gap == 0)[None, None]
    flat = active.reshape(B, H, n * n)
    idx = jnp.arange(n * n, dtype=jnp.int32)
    order = jnp.argsort(jnp.where(flat, idx, idx + n * n), axis=-1).astype(jnp.int32)
    count = jnp.sum(flat, axis=-1).astype(jnp.int32)
    order = jnp.take_along_axis(order, jnp.minimum(idx[None, None], count[..., None] - 1), axis=-1)
    return (order // n).reshape(-1), (order % n).reshape(-1), count.reshape(-1)


def _diff_attn(dq, dk, dv, scal, norm_g, q_norm, k_norm, ts=ATTN_TILE):
    B, S, _ = dq.shape
    ts = min(ts, S)
    assert S % ts == 0 and (2 * ts) % ATTN_ROW_CHUNK == 0 and ts <= ATTN_COL_RADIX * ATTN_COL_RADIX
    nk = S // ts
    n_pairs = nk * nk
    pair_q, pair_k, count = _attn_pair_lists(q_norm, k_norm, ts)
    col = jnp.arange(ts, dtype=jnp.int32)[:, None]
    lane = jnp.arange(LANES, dtype=jnp.int32)[None, :]
    kaug = jnp.where(lane < 3, col // ATTN_COL_RADIX * ATTN_COL_RADIX,
                     jnp.where(lane < 6, col % ATTN_COL_RADIX, 0)).astype(BF16)

    def tile_of(pairs, lag):
        def index_map(b, h, t, pq, pk, cnt):
            lists = b * DIFF_HEADS + h
            return b, pairs(pq, pk)[lists * n_pairs + jnp.clip(t - lag, 0, cnt[lists] - 1)], h
        return index_map

    query_tile = lambda pq, pk: pq
    key_tile = lambda pq, pk: pk
    const = lambda *shape: pl.BlockSpec(shape, lambda b, h, t, pq, pk, cnt: (0,) * len(shape))
    pair_buffer = lambda width, dtype: pltpu.VMEM((2, 2 * ts, width), dtype)
    grid_spec = pltpu.PrefetchScalarGridSpec(
        num_scalar_prefetch=3,
        grid=(B, DIFF_HEADS, n_pairs + ATTN_LAG),
        in_specs=[
            pl.BlockSpec(memory_space=pltpu.SMEM),
            pl.BlockSpec((1, ts, LANES), tile_of(query_tile, 0)),
            pl.BlockSpec((1, ts, LANES), tile_of(key_tile, 0)),
            pl.BlockSpec((1, ts, LANES), tile_of(key_tile, ATTN_LAG)),
            const(ts, LANES),
            const(1, DIFF_V_DIM),
        ],
        out_specs=pl.BlockSpec((1, ts, DIFF_V_DIM), tile_of(query_tile, ATTN_LAG)),
        scratch_shapes=[
            pltpu.VMEM((ATTN_SIDES, 2 * ts, 2 * LANES), BF16),
            pltpu.VMEM((ts, 2 * LANES), BF16),
            pltpu.VMEM((ts, 2 * LANES), BF16),
            pltpu.VMEM((2 * ts, ts), F32),
            pair_buffer(ts, F32),
            pair_buffer(LANES, F32),
            pltpu.VMEM((2 * ts, LANES), F32),
            pltpu.VMEM((2 * ts, 2 * DIFF_V_DIM), F32),
        ],
    )
    return pl.pallas_call(
        _diff_attn_kernel,
        grid_spec=grid_spec,
        out_shape=jax.ShapeDtypeStruct((B, S, DIFF_WIDTH), BF16),
        compiler_params=_params("parallel", "parallel", "arbitrary"),
        name="diff_attn",
    )(pair_q, pair_k, count, scal, dq, dk, dv, kaug, norm_g.reshape(1, DIFF_V_DIM))


def _out_proj_kernel(*refs, has_router):
    (x_ref, pool_ref, hgrn_ref, diff_ref, w_ref, g1_ref, ng_ref, sc_ref, sh_ref) = refs[:9]
    if has_router:
        rw_ref, rb_ref, xo_ref, h2_ref, comb_ref = refs[9:]
    else:
        xo_ref, h2_ref = refs[9:]
    a, b = POOL_WIDTH, POOL_WIDTH + HGRN_WIDTH
    mixed = (_dot(pool_ref[0], w_ref[0:a, :]) + _dot(hgrn_ref[0], w_ref[a:b, :])
             + _dot(diff_ref[0], w_ref[b:, :]))
    x = x_ref[0] + g1_ref[0] * mixed
    xo_ref[0] = x
    h2 = _rms_modulate(x, ng_ref[...], sc_ref[0], sh_ref[0])
    h2_ref[0] = h2.astype(BF16)
    if has_router:
        hh, hl = _split2(h2)
        wh, wl = _split2(rw_ref[...])
        logits = _dot_nt(wh, hh) + _dot_nt(wh, hl) + _dot_nt(wl, hh) + rb_ref[...]
        e_idx = lax.broadcasted_iota(jnp.int32, logits.shape, 0)
        m1 = jnp.max(logits, axis=0, keepdims=True)
        i1 = jnp.min(jnp.where(logits == m1, e_idx, N_EXPERTS), axis=0, keepdims=True)
        first = e_idx == i1
        rest = jnp.where(first, -jnp.inf, logits)
        m2 = jnp.max(rest, axis=0, keepdims=True)
        i2 = jnp.min(jnp.where(rest == m2, e_idx, N_EXPERTS), axis=0, keepdims=True)
        second = e_idx == i2
        e2 = jnp.exp(m2 - m1)
        p1 = 1.0 / (1.0 + e2)
        comb_ref[...] = jnp.where(first, p1, jnp.where(second, e2 * p1, 0.0))


def _out_proj(x, pool_o, hgrn_o, diff_o, w_out, g1, norm_g, sc, sh, router=None, tm=512):
    B, S, D = x.shape
    nt = S // tm
    has_router = router is not None
    row = lambda w: pl.BlockSpec((1, tm, w), lambda b, i: (b, i, 0))
    per_b = pl.BlockSpec((1, 1, D), lambda b, i: (b, 0, 0))
    const = lambda shape: pl.BlockSpec(shape, lambda b, i: (0,) * len(shape))
    in_specs = [row(D), row(POOL_WIDTH), row(HGRN_WIDTH), row(DIFF_WIDTH), const((D, D)),
                per_b, const((1, D)), per_b, per_b]
    args = [x, pool_o, hgrn_o, diff_o, w_out, g1, norm_g.reshape(1, D), sc, sh]
    out_specs = [row(D), row(D)]
    out_shape = [jax.ShapeDtypeStruct((B, S, D), F32), jax.ShapeDtypeStruct((B, S, D), BF16)]
    if has_router:
        rw_t, rb = router
        in_specs += [const((N_EXPERTS, D)), const((N_EXPERTS, 1))]
        args += [rw_t, rb]
        out_specs.append(pl.BlockSpec((N_EXPERTS, tm), lambda b, i: (0, b * nt + i)))
        out_shape.append(jax.ShapeDtypeStruct((N_EXPERTS, B * S), F32))
    return pl.pallas_call(
        functools.partial(_out_proj_kernel, has_router=has_router),
        grid=(B, nt),
        in_specs=in_specs, out_specs=out_specs, out_shape=out_shape,
        compiler_params=_params("parallel", "parallel"),
        name="out_proj",
    )(*args)


def _ffn_kernel(h_ref, wg_ref, wu_ref, wd_ref, o_ref):
    @pl.when(pl.program_id(1) == 0)
    def _():
        o_ref[...] = jnp.zeros_like(o_ref)

    h = h_ref[...]
    mid = _silu(_dot(h, wg_ref[...])) * _dot(h, wu_ref[...])
    o_ref[...] += _dot(mid.astype(BF16), wd_ref[...])


def _ffn(h2, wg, wu, wd, tm=1024, tf=256):
    T, D = h2.shape
    F = wg.shape[-1]
    assert T % tm == 0 and F % tf == 0
    return pl.pallas_call(
        _ffn_kernel,
        grid=(T // tm, F // tf),
        in_specs=[
            pl.BlockSpec((tm, D), lambda i, f: (i, 0)),
            pl.BlockSpec((D, tf), lambda i, f: (0, f)),
            pl.BlockSpec((D, tf), lambda i, f: (0, f)),
            pl.BlockSpec((tf, D), lambda i, f: (f, 0)),
        ],
        out_specs=pl.BlockSpec((tm, D), lambda i, f: (i, 0)),
        out_shape=jax.ShapeDtypeStruct((T, D), F32),
        compiler_params=_params("parallel", "arbitrary"),
        name="ffn",
    )(h2, wg, wu, wd)


MOE_SUB = 256
MOE_TAIL = 128
MOE_RANK_BLOCK = 512


def _moe_kernel(h_ref, comb_ref, wg_ref, wu_ref, wd_ref, o_ref, rank_sc, cnt_sc, xc_sc, yc_sc):
    e = pl.program_id(1)
    fi = pl.program_id(2)
    nf = pl.num_programs(2)
    tm = h_ref.shape[0]

    @pl.when((e == 0) & (fi == 0))
    def _():
        o_ref[...] = jnp.zeros_like(o_ref)
        s_i = lax.broadcasted_iota(jnp.int32, (MOE_RANK_BLOCK, MOE_RANK_BLOCK), 0)
        t_i = lax.broadcasted_iota(jnp.int32, (MOE_RANK_BLOCK, MOE_RANK_BLOCK), 1)
        before = (s_i < t_i).astype(BF16)
        base = jnp.zeros((N_EXPERTS, 1), F32)
        for c0 in range(0, tm, MOE_RANK_BLOCK):
            ind = (comb_ref[:, c0:c0 + MOE_RANK_BLOCK] > 0.0).astype(BF16)
            rank_sc[:, c0:c0 + MOE_RANK_BLOCK] = _dot(ind, before) + base
            base = base + jnp.sum(ind.astype(F32), axis=1, keepdims=True)
        for ex in range(N_EXPERTS):
            cnt_sc[ex] = jnp.sum(base[ex:ex + 1, :]).astype(jnp.int32)

    n_rows = cnt_sc[e]
    n_full = n_rows // MOE_SUB
    rest = n_rows - n_full * MOE_SUB
    n_sub = n_full + (rest > MOE_TAIL).astype(jnp.int32)
    has_tail = (rest > 0) & (rest <= MOE_TAIL)
    sel = comb_ref[pl.ds(e, 1), :]
    rank = rank_sc[pl.ds(e, 1), :]

    def for_blocks(fn):
        def body(sub, carry):
            fn(pl.multiple_of(sub * MOE_SUB, MOE_SUB), MOE_SUB)
            return carry
        lax.fori_loop(0, n_sub, body, 0)

        @pl.when(has_tail)
        def _():
            fn(pl.multiple_of(n_full * MOE_SUB, MOE_SUB), MOE_TAIL)

    def onehot(r0, size):
        slot = lax.broadcasted_iota(jnp.int32, (size, tm), 0) + r0
        return (rank == slot.astype(F32)) & (sel > 0.0)

    @pl.when(fi == 0)
    def _():
        def gather(r0, size):
            g = onehot(r0, size).astype(BF16)
            xc_sc[pl.ds(r0, size), :] = _dot(g, h_ref[...]).astype(BF16)
            yc_sc[pl.ds(r0, size), :] = jnp.zeros((size, yc_sc.shape[1]), F32)
        for_blocks(gather)

    def expert(r0, size):
        xs = xc_sc[pl.ds(r0, size), :]
        mid = _silu(_dot(xs, wg_ref[0])) * _dot(xs, wu_ref[0])
        yc_sc[pl.ds(r0, size), :] += _dot(mid.astype(BF16), wd_ref[0])
    for_blocks(expert)

    @pl.when(fi == nf - 1)
    def _():
        def scatter(r0, size):
            hit = onehot(r0, size)
            weight = jnp.sum(jnp.where(hit, sel, 0.0), axis=1, keepdims=True)
            ys = (yc_sc[pl.ds(r0, size), :] * weight).astype(BF16)
            o_ref[...] = (o_ref[...].astype(F32) + _dot_tn(hit.astype(BF16), ys)).astype(o_ref.dtype)
        for_blocks(scatter)


def _moe(h2, comb, wg, wu, wd, tm=2048, tf=1408):
    T, D = h2.shape
    F = wg.shape[-1]
    assert T % tm == 0 and F % tf == 0
    return pl.pallas_call(
        _moe_kernel,
        grid=(T // tm, N_EXPERTS, F // tf),
        in_specs=[
            pl.BlockSpec((tm, D), lambda i, e, f: (i, 0)),
            pl.BlockSpec((N_EXPERTS, tm), lambda i, e, f: (0, i)),
            pl.BlockSpec((1, D, tf), lambda i, e, f: (e, 0, f)),
            pl.BlockSpec((1, D, tf), lambda i, e, f: (e, 0, f)),
            pl.BlockSpec((1, tf, D), lambda i, e, f: (e, f, 0)),
        ],
        out_specs=pl.BlockSpec((tm, D), lambda i, e, f: (i, 0)),
        out_shape=jax.ShapeDtypeStruct((T, D), BF16),
        scratch_shapes=[
            pltpu.VMEM((N_EXPERTS, tm), F32),
            pltpu.SMEM((N_EXPERTS,), jnp.int32),
            pltpu.VMEM((tm, D), BF16),
            pltpu.VMEM((tm, D), F32),
        ],
        compiler_params=_params("parallel", "arbitrary", "arbitrary"),
        name="moe",
    )(h2, comb, wg, wu, wd)


def _final_kernel(x_ref, f_ref, g2_ref, ng_ref, o_ref):
    x = x_ref[0] + g2_ref[0] * f_ref[0].astype(F32)
    o_ref[0] = x * lax.rsqrt(jnp.mean(x * x, axis=-1, keepdims=True) + RMS_EPS) * ng_ref[...]


def _final(x, f, g2, norm_g, tm=1024):
    B, S, D = x.shape
    row = pl.BlockSpec((1, tm, D), lambda b, i: (b, i, 0))
    return pl.pallas_call(
        _final_kernel,
        grid=(B, S // tm),
        in_specs=[row, row, pl.BlockSpec((1, 1, D), lambda b, i: (b, 0, 0)),
                  pl.BlockSpec((1, D), lambda b, i: (0, 0))],
        out_specs=row,
        out_shape=jax.ShapeDtypeStruct((B, S, D), F32),
        compiler_params=_params("parallel", "parallel"),
        name="final_norm",
    )(x, f, g2, norm_g.reshape(1, D))


def _block_diag(blocks):
    n, r, c = blocks.shape
    out = jnp.zeros((n * r, n * c), blocks.dtype)
    for i in range(n):
        out = out.at[i * r:(i + 1) * r, i * c:(i + 1) * c].set(blocks[i])
    return out


def _trunk(x, ada, prm):
    B, S, D = x.shape
    f = None
    g2 = None
    for l in range(DEPTH):
        sh1, sc1, g1, sh2, sc2, g2_l = [ada[l][:, None, k * D:(k + 1) * D] for k in range(6)]
        x, (u, hg, dq, dk, dv, sq_norms) = _norm_proj(x, f, g2, prm["norm1_g"][l], sc1, sh1, prm["w_in"][l])
        q_norm, k_norm = _tile_norms(sq_norms, S, ATTN_TILE)
        pool_o = _pool(u, prm["pool_bd"][l], prm["pool_scale"][l])
        hgrn_o = _hgrn(hg, prm["lbs"][l, 0], prm["lbs"][l, 1], prm["hgrn_norm_g"][l],
                       prm["ones_bd"], prm["mask_bd"])
        diff_o = _diff_attn(dq, dk, dv, prm["diff_scal"][l], prm["diff_norm_g"][l], q_norm, k_norm)
        moe = l % 2 == 1
        j = l // 2
        router = (prm["router_wt"][j], prm["router_b"][j]) if moe else None
        res = _out_proj(x, pool_o, hgrn_o, diff_o, prm["w_out"][l], g1, prm["norm2_g"][l], sc2, sh2,
                        router=router)
        x = res[0]
        h2 = res[1].reshape(B * S, D)
        if moe:
            f = _moe(h2, res[2], prm["moe_wg"][j], prm["moe_wu"][j], prm["moe_wd"][j])
        else:
            f = _ffn(h2, prm["ffn_wg"][j], prm["ffn_wu"][j], prm["ffn_wd"][j])
        f = f.reshape(B, S, D)
        g2 = g2_l
    return _final(x, f, g2, prm["final_norm_g"])


def kernel(x_prompt, x_sample, c_prompt, c_sample, ada_w, ada_b, norm1_g, norm2_g, w_in, pool_w, pool_scale, hgrn_lb, hgrn_norm_g, diff_lambda, diff_norm_g, w_out, ffn_w_gate, ffn_w_up, ffn_w_down, router_w, router_b, moe_w_gate, moe_w_up, moe_w_down, final_norm_g):
    nb_p, nb_s = c_prompt.shape[0], c_sample.shape[0]
    rows = -(-(nb_p + nb_s) // 8) * 8
    c_all = jnp.zeros((rows, D_MODEL), F32).at[:nb_p].set(c_prompt).at[nb_p:nb_p + nb_s].set(c_sample)
    ada = _ada(c_all, ada_w, ada_b)

    p_lb = jax.nn.softmax(hgrn_lb.astype(F32), axis=0)
    lbs = jnp.cumsum(p_lb, axis=0) - p_lb[0:1]
    lv = diff_lambda.astype(F32)
    lam_init = jnp.asarray([0.8 - 0.6 * math.exp(-0.3 * l) for l in range(DEPTH)], F32)
    lam = jnp.exp(jnp.sum(lv[:, 0] * lv[:, 1], axis=-1)) - jnp.exp(jnp.sum(lv[:, 2] * lv[:, 3], axis=-1)) + lam_init
    slopes = jnp.asarray([2.0 ** (-8.0 * (h + 1) / DIFF_HEADS) for h in range(DIFF_HEADS)], F32)
    diff_scal = jnp.concatenate(
        [lam[:, None], (1.0 - lam_init)[:, None], jnp.broadcast_to(slopes, (DEPTH, DIFF_HEADS)),
         jnp.zeros((DEPTH, 2), F32)], axis=1)
    head_blocks = jnp.ones((HGRN_WIDTH // HGRN_HEAD_DIM, HGRN_HEAD_DIM, HGRN_HEAD_DIM), F32)
    prm = {
        "norm1_g": norm1_g, "norm2_g": norm2_g, "final_norm_g": final_norm_g,
        "w_in": w_in.astype(BF16), "w_out": w_out.astype(BF16),
        "pool_bd": jnp.stack([_block_diag(pool_w[l]) for l in range(DEPTH)]).astype(BF16),
        "pool_scale": pool_scale,
        "lbs": lbs,
        "hgrn_norm_g": jnp.tile(hgrn_norm_g, (1, HGRN_WIDTH // HGRN_HEAD_DIM)),
        "ones_bd": _block_diag(head_blocks).astype(BF16),
        "mask_bd": _block_diag(head_blocks),
        "diff_scal": diff_scal, "diff_norm_g": diff_norm_g,
        "ffn_wg": ffn_w_gate.astype(BF16), "ffn_wu": ffn_w_up.astype(BF16), "ffn_wd": ffn_w_down.astype(BF16),
        "router_wt": jnp.swapaxes(router_w, 1, 2), "router_b": router_b[:, :, None],
        "moe_wg": moe_w_gate.astype(BF16), "moe_wu": moe_w_up.astype(BF16), "moe_wd": moe_w_down.astype(BF16),
    }
    y_prompt = _trunk(x_prompt, ada[:, :nb_p], prm)
    y_sample = _trunk(x_sample, ada[:, nb_p:nb_p + nb_s], prm)
    return (y_prompt, y_sample)
```

```python
import functools
import math

import jax
import jax.numpy as jnp
from jax import lax
from jax.experimental import pallas as pl
from jax.experimental.pallas import tpu as pltpu

D_MODEL = 1024
DEPTH = 4
POOL_GROUP_DIM = 64
POOL_WIDTH = 256
POOL_WINDOWS = (2, 4, 8, 16)
POOL_HALO = 16
HGRN_HEAD_DIM = 64
HGRN_WIDTH = 256
DIFF_HEADS = 4
DIFF_HEAD_DIM = 64
DIFF_V_DIM = 128
DIFF_QK_WIDTH = 512
DIFF_WIDTH = 512
D_FF = 2816
N_EXPERTS = 8
RMS_EPS = 1e-6

LANES = 128
VMEM_LIMIT_BYTES = 56 * 1024 * 1024

F32 = jnp.float32
BF16 = jnp.bfloat16


def _params(*semantics):
    return pltpu.CompilerParams(dimension_semantics=semantics, vmem_limit_bytes=VMEM_LIMIT_BYTES)


def _split2(a):
    hi = a.astype(BF16)
    lo = (a - hi.astype(F32)).astype(BF16)
    return hi, lo


def _split3(a):
    hi = a.astype(BF16)
    r = a - hi.astype(F32)
    mid = r.astype(BF16)
    lo = (r - mid.astype(F32)).astype(BF16)
    return hi, mid, lo


def _dot(a, b):
    return jnp.dot(a, b, preferred_element_type=F32)


def _dot_nt(a, b):
    return lax.dot_general(a, b, (((1,), (1,)), ((), ())), preferred_element_type=F32)


def _dot_tn(a, b):
    return lax.dot_general(a, b, (((0,), (0,)), ((), ())), preferred_element_type=F32)


def _dot_f32(a, b):
    ah, al = _split2(a)
    bh, bl = _split2(b)
    return _dot(ah, bh) + _dot(ah, bl) + _dot(al, bh)


def _silu(a):
    return a * jax.nn.sigmoid(a)


def _rms_modulate(x, norm_g, scale, shift):
    y = x * lax.rsqrt(jnp.mean(x * x, axis=-1, keepdims=True) + RMS_EPS) * norm_g
    return y * (1.0 + scale) + shift


def _ada_kernel(c_ref, w_ref, b_ref, o_ref):
    o_ref[0] = _dot_f32(_silu(c_ref[...]), w_ref[0]) + b_ref[0]


def _ada(c_all, ada_w, ada_b):
    rows, d = c_all.shape
    tn = 1024
    n_out = ada_w.shape[-1]
    return pl.pallas_call(
        _ada_kernel,
        grid=(DEPTH, n_out // tn),
        in_specs=[
            pl.BlockSpec((rows, d), lambda l, n: (0, 0)),
            pl.BlockSpec((1, d, tn), lambda l, n: (l, 0, n)),
            pl.BlockSpec((1, 1, tn), lambda l, n: (l, 0, n)),
        ],
        out_specs=pl.BlockSpec((1, rows, tn), lambda l, n: (l, 0, n)),
        out_shape=jax.ShapeDtypeStruct((DEPTH, rows, n_out), F32),
        compiler_params=_params("arbitrary", "arbitrary"),
        name="ada",
    )(c_all, ada_w, ada_b.reshape(DEPTH, 1, n_out))


_PROJ_SPLITS = (
    (0, POOL_WIDTH),
    (POOL_WIDTH, POOL_WIDTH + 5 * HGRN_WIDTH),
    (1536, 2048),
    (2048, 2560),
    (2560, 3072),
)
_PROJ_QK = (2, 3)


def _norm_proj_kernel(*refs, has_f):
    if has_f:
        x_ref, f_ref, g2_ref, ng_ref, sc_ref, sh_ref, w_ref, seg_ref, xo_ref = refs[:9]
        outs = refs[9:]
        x = x_ref[0] + g2_ref[0] * f_ref[0].astype(F32)
        xo_ref[0] = x
    else:
        x_ref, ng_ref, sc_ref, sh_ref, w_ref, seg_ref = refs[:6]
        outs = refs[6:]
        x = x_ref[0]
    hb = _rms_modulate(x, ng_ref[...], sc_ref[0], sh_ref[0]).astype(BF16)
    norms = []
    for k, (o_ref, (a, b)) in enumerate(zip(outs, _PROJ_SPLITS)):
        out = _dot(hb, w_ref[:, a:b]).astype(BF16)
        o_ref[0] = out
        if k in _PROJ_QK:
            sq = out.astype(F32)
            per_row = _dot((sq * sq).astype(BF16), seg_ref[...])
            norms.append(jnp.max(per_row, axis=0, keepdims=True))
    outs[-1][0, 0] = jnp.concatenate(norms, axis=0)


def _norm_proj(x, f, g2, norm_g, sc, sh, w_in, tm=512):
    B, S, D = x.shape
    has_f = f is not None
    row = pl.BlockSpec((1, tm, D), lambda b, i: (b, i, 0))
    per_b = pl.BlockSpec((1, 1, D), lambda b, i: (b, 0, 0))
    in_specs = [row]
    args = [x]
    if has_f:
        in_specs += [row, per_b]
        args += [f, g2]
    seg = (jnp.arange(DIFF_QK_WIDTH)[:, None] // DIFF_HEAD_DIM == jnp.arange(LANES)[None, :]).astype(BF16)
    in_specs += [pl.BlockSpec((1, D), lambda b, i: (0, 0)), per_b, per_b,
                 pl.BlockSpec(w_in.shape, lambda b, i: (0, 0)),
                 pl.BlockSpec(seg.shape, lambda b, i: (0, 0))]
    args += [norm_g.reshape(1, D), sc, sh, w_in, seg]
    out_specs, out_shape = [], []
    if has_f:
        out_specs.append(row)
        out_shape.append(jax.ShapeDtypeStruct((B, S, D), F32))
    for a, b_ in _PROJ_SPLITS:
        out_specs.append(pl.BlockSpec((1, tm, b_ - a), lambda b, i: (b, i, 0)))
        out_shape.append(jax.ShapeDtypeStruct((B, S, b_ - a), BF16))
    out_specs.append(pl.BlockSpec((1, 1, len(_PROJ_QK), LANES), lambda b, i: (b, i, 0, 0)))
    out_shape.append(jax.ShapeDtypeStruct((B, S // tm, len(_PROJ_QK), LANES), F32))
    res = pl.pallas_call(
        functools.partial(_norm_proj_kernel, has_f=has_f),
        grid=(B, S // tm),
        in_specs=in_specs, out_specs=out_specs, out_shape=out_shape,
        compiler_params=_params("parallel", "parallel"),
        name="norm_proj",
    )(*args)
    if has_f:
        return res[0], res[1:]
    return x, res


def _tile_norms(sq_norms, seq_len, ts):
    B, n = sq_norms.shape[:2]
    per_head = sq_norms[..., :2 * DIFF_HEADS].reshape(B, n, len(_PROJ_QK), DIFF_HEADS, 2).max(axis=-1)
    ts = min(ts, seq_len)
    per_tile = per_head.reshape(B, seq_len // ts, n // (seq_len // ts), len(_PROJ_QK), DIFF_HEADS).max(axis=2)
    bound = jnp.sqrt(per_tile).transpose(2, 0, 3, 1)
    return bound[0], bound[1]


def _pool_kernel(prev_ref, u_ref, next_ref, w_ref, scale_ref, o_ref, *, seq_len):
    i = pl.program_id(1)
    n = pl.num_programs(1)
    tm = u_ref.shape[1]
    u = u_ref[0].astype(F32)
    prev = jnp.where(i > 0, prev_ref[0].astype(F32), 0.0)
    nxt = jnp.where(i < n - 1, next_ref[0].astype(F32), 0.0)
    ext = jnp.concatenate([prev, u, nxt], axis=0)
    rows = ext.shape[0]
    group = lax.broadcasted_iota(jnp.int32, (tm, POOL_WIDTH), 1) // POOL_GROUP_DIM
    s = ext
    win = jnp.zeros((tm, POOL_WIDTH), F32)
    for gi, w in enumerate(POOL_WINDOWS):
        s = s + pltpu.roll(s, w // 2, axis=0)
        lead = w // 2 - 1
        centred = s if lead == 0 else pltpu.roll(s, rows - lead, axis=0)
        win = jnp.where(group == gi, centred[POOL_HALO:POOL_HALO + tm], win)
    t = i * tm + lax.broadcasted_iota(jnp.int32, (tm, POOL_WIDTH), 0)
    half = jnp.left_shift(1, group)
    count = jnp.minimum(t + half, seq_len) - jnp.maximum(t - half, 0)
    d = win / count.astype(F32) - u
    o_ref[0] = (_dot(d.astype(BF16), w_ref[...]) * scale_ref[...]).astype(BF16)


def _pool(u, w_bd, scale, tm=512):
    B, S, W = u.shape
    hb = tm // POOL_HALO
    nh = S // POOL_HALO
    return pl.pallas_call(
        functools.partial(_pool_kernel, seq_len=S),
        grid=(B, S // tm),
        in_specs=[
            pl.BlockSpec((1, POOL_HALO, W), lambda b, i: (b, jnp.maximum(i * hb - 1, 0), 0)),
            pl.BlockSpec((1, tm, W), lambda b, i: (b, i, 0)),
            pl.BlockSpec((1, POOL_HALO, W), lambda b, i: (b, jnp.minimum((i + 1) * hb, nh - 1), 0)),
            pl.BlockSpec((W, W), lambda b, i: (0, 0)),
            pl.BlockSpec((1, W), lambda b, i: (0, 0)),
        ],
        out_specs=pl.BlockSpec((1, tm, W), lambda b, i: (b, i, 0)),
        out_shape=jax.ShapeDtypeStruct((B, S, W), BF16),
        compiler_params=_params("parallel", "parallel"),
        name="pool",
    )(u, u, u, w_bd, scale.reshape(1, W))


HGRN_SUB = 16
HGRN_UNROLL = 16


def _hgrn_kernel(qf_ref, zf_ref, vf_ref, qb_ref, zb_ref, vb_ref, lbf_ref, lbb_ref, ones_ref, mask_ref,
                 perm_ref, unperm_ref, of_ref, ob_ref, *scratch):
    c = HGRN_SUB
    tt = qf_ref.shape[1]
    n_sub = tt // c
    ones_bd = ones_ref[...]
    half = len(scratch) // 2
    directions = (
        (False, qf_ref, zf_ref, vf_ref, lbf_ref, of_ref) + tuple(scratch[:half]),
        (True, qb_ref, zb_ref, vb_ref, lbb_ref, ob_ref) + tuple(scratch[half:]),
    )

    @pl.when(pl.program_id(1) == 0)
    def _():
        for d in directions:
            d[6][...] = jnp.zeros_like(d[6])

    r_i = lax.broadcasted_iota(jnp.int32, (tt, tt), 0)
    s_i = lax.broadcasted_iota(jnp.int32, (tt, tt), 1)
    same_sub = (r_i % n_sub) == (s_i % n_sub)
    blk = same_sub.astype(BF16)

    for rev, q_ref, z_ref, v_ref, lb_ref, _, _, qe_sc, kt_sc, vb_sc, gam_sc, acc_sc in directions:
        perm = perm_ref[...]
        q = _silu(_dot(perm, q_ref[0]))
        z = _dot(perm, z_ref[0])
        v = _dot(perm, v_ref[0])
        lb = lb_ref[...]
        f = jnp.maximum(lb + (1.0 - lb) * jax.nn.sigmoid(z), 1e-30)
        g = jnp.log(f)
        kk = (1.0 - lb) * jax.nn.sigmoid(-z)

        tri = (same_sub & ((s_i // n_sub >= r_i // n_sub) if rev
                           else (s_i // n_sub <= r_i // n_sub))).astype(BF16)
        parts = _split3(g)
        cum = sum(_dot(tri, p) for p in parts)
        tot = sum(_dot(blk, p) for p in parts)

        unperm = unperm_ref[...]
        qe_sc[...] = _dot(unperm, (q * jnp.exp(cum)).astype(BF16)).astype(BF16)
        kt_sc[...] = _dot(unperm, (kk * jnp.exp(tot - cum)).astype(BF16)).astype(BF16)
        vb_sc[...] = v_ref[0]
        gam_sc[...] = jnp.exp(tot[0:n_sub])

        kd = kk
        acc_sc[...] = _dot((q * kk).astype(BF16), ones_bd) * v
        for delta in range(1, c):
            n = tt - n_sub * delta
            tgt = slice(0, n) if rev else slice(n_sub * delta, tt)
            src = slice(n_sub * delta, tt) if rev else slice(0, n)
            kd = (kd[n_sub:] if rev else kd[:n]) * f[tgt]
            acc_sc[tgt] += _dot((q[tgt] * kd).astype(BF16), ones_bd) * v[src]
        acc_sc[...] = sum(_dot(unperm, p) for p in _split3(acc_sc[...]))

    mask = mask_ref[0:LANES, 0:LANES]

    def body(ci, carry):
        for rev, _, _, _, _, _, st_ref, qe_sc, kt_sc, vb_sc, gam_sc, acc_sc in directions:
            idx = (n_sub - 1 - ci) if rev else ci
            r0 = pl.multiple_of(idx * c, c)
            gam = gam_sc[pl.ds(idx, 1), :]
            for lanes in (slice(0, LANES), slice(LANES, 2 * LANES)):
                st = st_ref[lanes, lanes]
                acc_sc[pl.ds(r0, c), lanes] += _dot_nt(qe_sc[pl.ds(r0, c), lanes], st.astype(BF16))
                kv = _dot_tn(vb_sc[pl.ds(r0, c), lanes], kt_sc[pl.ds(r0, c), lanes])
                st_ref[lanes, lanes] = st * gam[:, lanes] + kv * mask
        return carry

    lax.fori_loop(0, n_sub, body, 0, unroll=HGRN_UNROLL)

    for d in directions:
        d[5][0] = d[11][...]


def _hgrn_finish_kernel(of_ref, ob_ref, gate_ref, ng_ref, ones_ref, o_ref):
    o = of_ref[0] + ob_ref[0]
    ms = sum(_dot(p, ones_ref[...]) for p in _split2(o * o)) * (1.0 / HGRN_HEAD_DIM)
    o = o * lax.rsqrt(ms + RMS_EPS) * ng_ref[...] * _silu(gate_ref[0].astype(F32))
    o_ref[0] = o.astype(BF16)


def _hgrn(hg, lb_fwd, lb_bwd, norm_g, ones_bd, mask_bd, tt=256, tm=1024):
    B, S, _ = hg.shape
    W = HGRN_WIDTH
    nt = S // tt

    def col(k, rev):
        return pl.BlockSpec((1, tt, W), lambda b, j: (b, nt - 1 - j if rev else j, k))

    const = lambda shape: pl.BlockSpec(shape, lambda b, j: (0, 0))
    per_direction = [
        pltpu.VMEM((W, W), F32),
        pltpu.VMEM((tt, W), BF16),
        pltpu.VMEM((tt, W), BF16),
        pltpu.VMEM((tt, W), BF16),
        pltpu.VMEM((tt // HGRN_SUB, W), F32),
        pltpu.VMEM((tt, W), F32),
    ]
    token = jnp.arange(tt)
    perm = (token[None, :] == (token % (tt // HGRN_SUB) * HGRN_SUB + token // (tt // HGRN_SUB))[:, None])
    perm = perm.astype(BF16)
    o_fwd, o_bwd = pl.pallas_call(
        _hgrn_kernel,
        grid=(B, nt),
        in_specs=[col(0, False), col(1, False), col(3, False), col(0, True), col(2, True), col(3, True),
                  const((1, W)), const((1, W)), const((W, W)), const((W, W)),
                  const((tt, tt)), const((tt, tt))],
        out_specs=[col(0, False), col(0, True)],
        out_shape=[jax.ShapeDtypeStruct((B, S, W), F32)] * 2,
        scratch_shapes=per_direction * 2,
        compiler_params=_params("parallel", "arbitrary"),
        name="hgrn",
    )(hg, hg, hg, hg, hg, hg, lb_fwd.reshape(1, W), lb_bwd.reshape(1, W), ones_bd, mask_bd, perm, perm.T)

    tm = min(tm, S)
    row = pl.BlockSpec((1, tm, W), lambda b, i: (b, i, 0))
    return pl.pallas_call(
        _hgrn_finish_kernel,
        grid=(B, S // tm),
        in_specs=[row, row, pl.BlockSpec((1, tm, W), lambda b, i: (b, i, 4)),
                  pl.BlockSpec((1, W), lambda b, i: (0, 0)), pl.BlockSpec((W, W), lambda b, i: (0, 0))],
        out_specs=row,
        out_shape=jax.ShapeDtypeStruct((B, S, W), BF16),
        compiler_params=_params("parallel", "parallel"),
        name="hgrn_finish",
    )(o_fwd, o_bwd, hg, norm_g.reshape(1, W), ones_bd)


LOG2E = 1.4426950408889634
ATTN_COL_RADIX = 32
ATTN_SIDES = 3
ATTN_ROW_CHUNK = 256
ATTN_LAG = 1
ATTN_TILE = 1024


def _diff_attn_kernel(pq_ref, pk_ref, cnt_ref, scal_ref, q_ref, k_ref, v_ref, kaug_ref, ng_ref, o_ref,
                      qx_sc, kf_sc, vf_sc, bias_sc, s_sc, pmax_sc, m_sc, acc_sc):
    h = pl.program_id(0)
    t = pl.program_id(2)
    lists = pl.program_id(1) * pl.num_programs(0) + h
    n_pairs = cnt_ref[lists]
    base = lists * (pl.num_programs(2) - ATTN_LAG)
    tq = q_ref.shape[1]
    tk = k_ref.shape[1]
    sigma = scal_ref[2 + h] * LOG2E

    def pair_at(step):
        i = base + jnp.clip(step, 0, n_pairs - 1)
        return pq_ref[i], pk_ref[i]

    qi, ki = pair_at(t)
    qi_done, ki_done = pair_at(t - 1)
    new_query_tile = (t == 0) | (qi != qi_done)
    done_first_of_tile = (t <= 1) | (qi_done != pair_at(t - 2)[0])
    done_last_of_tile = (t >= 1) & (t <= n_pairs) & ((t == n_pairs) | (qi != qi_done))
    chunks = range(0, 2 * tq, ATTN_ROW_CHUNK)

    def sigma_digits(lane_idx, first_lane):
        hi, mid, lo = [p.astype(F32) for p in _split3(jnp.full(lane_idx.shape, sigma, F32))]
        part = (lane_idx - first_lane) % 3
        digits = jnp.where(part == 0, hi, jnp.where(part == 1, mid, lo))
        return jnp.where((lane_idx >= first_lane) & (lane_idx < first_lane + 6), digits, 0.0)

    @pl.when((pl.program_id(1) == 0) & (t == 0))
    def _():
        vcol = lax.broadcasted_iota(jnp.int32, (tk, LANES), 1)
        kf_sc[:, LANES:2 * LANES] = jnp.where(
            vcol < 6, kaug_ref[...].astype(F32), sigma_digits(vcol, 6)).astype(BF16)
        vf_sc[:, LANES:2 * LANES] = jnp.where(vcol == 0, 1.0, 0.0).astype(BF16)
        for r0 in chunks:
            rr = (r0 + lax.broadcasted_iota(jnp.int32, (ATTN_ROW_CHUNK, tk), 0)) % tq
            cc = lax.broadcasted_iota(jnp.int32, (ATTN_ROW_CHUNK, tk), 1)
            bias_sc[r0:r0 + ATTN_ROW_CHUNK] = -sigma * jnp.abs(rr - cc).astype(F32)
        s_sc[1] = jnp.zeros(s_sc.shape[1:], F32)
        pmax_sc[1] = jnp.zeros(pmax_sc.shape[1:], F32)
        m_sc[...] = jnp.zeros(m_sc.shape, F32)
        acc_sc[...] = jnp.zeros(acc_sc.shape, F32)

    @pl.when(new_query_tile)
    def _():
        q = q_ref[0].astype(F32) * (DIFF_HEAD_DIM ** -0.5 * LOG2E)
        lane = lax.broadcasted_iota(jnp.int32, q.shape, 1)
        first = lane < DIFF_HEAD_DIM
        qa = jnp.where(first, q, 0.0).astype(BF16)
        qb = jnp.where(first, 0.0, q).astype(BF16)
        row = lax.broadcasted_iota(jnp.int32, q.shape, 0)
        row_digits = jnp.where(lane < 9, row // ATTN_COL_RADIX * ATTN_COL_RADIX, row % ATTN_COL_RADIX)
        aug = sigma_digits(lane, 0) - jnp.where((lane >= 6) & (lane < 12), row_digits.astype(F32), 0.0)
        for side, a in enumerate((aug, -aug, jnp.zeros_like(aug))):
            qx_sc[side, 0:tq, 0:LANES] = qa
            qx_sc[side, tq:2 * tq, 0:LANES] = qb
            qx_sc[side, 0:tq, LANES:2 * LANES] = a.astype(BF16)
            qx_sc[side, tq:2 * tq, LANES:2 * LANES] = a.astype(BF16)

    n_col = tk // LANES

    kf_sc[:, 0:LANES] = k_ref[0]
    vf_sc[:, 0:LANES] = v_ref[0]
    side = jnp.where(ki < qi, 0, jnp.where(ki > qi, 1, 2))

    def score(overlapping, cur):
        for r0 in chunks:
            rows = slice(r0, r0 + ATTN_ROW_CHUNK)
            s = _dot_nt(qx_sc[side, rows, :], kf_sc[...])
            if overlapping:
                s = s + bias_sc[rows]
            s_sc[cur, rows] = s
            pmax_sc[cur, rows] = functools.reduce(
                jnp.maximum, [s[:, c * LANES:(c + 1) * LANES] for c in range(n_col)])

    def accumulate(old):
        tile_const = -sigma * (jnp.abs(qi_done - ki_done) * tq).astype(F32)
        for r0 in chunks:
            rows = slice(r0, r0 + ATTN_ROW_CHUNK)
            m_old = jnp.where(done_first_of_tile, -jnp.inf, m_sc[rows])
            row_max = jnp.max(pmax_sc[old, rows], axis=-1, keepdims=True) + tile_const
            m_new = jnp.maximum(m_old, row_max)
            alpha = jnp.exp2(m_old - m_new)
            shift = jnp.concatenate([m_new - tile_const] * n_col, axis=1)
            p = jnp.exp2((s_sc[old, rows] - shift).astype(BF16))
            acc_sc[rows] = jnp.concatenate([alpha, alpha], axis=1) * acc_sc[rows] + _dot(p, vf_sc[...])
            m_sc[rows] = m_new

    for parity in (0, 1):
        for overlapping in (False, True):
            @pl.when((t <= n_pairs) & (t % 2 == parity) & ((side == 2) == overlapping))
            def _():
                accumulate(1 - parity)
                score(overlapping, parity)

    @pl.when(done_last_of_tile)
    def _():
        lam = scal_ref[0]
        out_scale = scal_ref[1]
        acc = acc_sc[...]
        o_all = acc[:, 0:DIFF_V_DIM] / acc[:, DIFF_V_DIM:DIFF_V_DIM + 1]
        o = o_all[0:tq] - lam * o_all[tq:2 * tq]
        o = o * lax.rsqrt(jnp.mean(o * o, axis=-1, keepdims=True) + RMS_EPS)
        o_ref[0] = (o * ng_ref[...] * out_scale).astype(BF16)


ATTN_SKIP_MARGIN = 60.0
ATTN_NORM_SLACK = 1.02
ALIBI_SLOPES = tuple(2.0 ** (-8.0 * (h + 1) / DIFF_HEADS) for h in range(DIFF_HEADS))


def _attn_pair_lists(q_norm, k_norm, ts):
    B, H, n = q_norm.shape
    scale = ATTN_NORM_SLACK * DIFF_HEAD_DIM ** -0.5
    i = jnp.arange(n)
    gap = jnp.abs(i[:, None] - i[None, :])
    closest = jnp.where(gap == 0, 0, (gap - 1) * ts + 1).astype(F32)
    slopes = jnp.asarray(ALIBI_SLOPES, F32)[None, :, None, None]
    best = scale * q_norm[:, :, :, None] * k_norm[:, :, None, :] - slopes * closest[None, None]
    floor = -scale * q_norm * k_norm
    active = (best >= floor[:, :, :, None] - ATTN_SKIP_MARGIN) | (gap == 0)[None, None]
    flat = active.reshape(B, H, n * n)
    idx = jnp.arange(n * n, dtype=jnp.int32)
    order = jnp.argsort(jnp.where(flat, idx, idx + n * n), axis=-1).astype(jnp.int32)
    count = jnp.sum(flat, axis=-1).astype(jnp.int32)
    order = jnp.take_along_axis(order, jnp.minimum(idx[None, None], count[..., None] - 1), axis=-1)
    return (order // n).reshape(-1), (order % n).reshape(-1), count.reshape(-1)


def _diff_attn(dq, dk, dv, scal, norm_g, q_norm, k_norm, ts=ATTN_TILE):
    B, S, _ = dq.shape
    ts = min(ts, S)
    assert S % ts == 0 and (2 * ts) % ATTN_ROW_CHUNK == 0 and ts <= ATTN_COL_RADIX * ATTN_COL_RADIX
    nk = S // ts
    n_pairs = nk * nk
    pair_q, pair_k, count = _attn_pair_lists(q_norm, k_norm, ts)
    col = jnp.arange(ts, dtype=jnp.int32)[:, None]
    lane = jnp.arange(LANES, dtype=jnp.int32)[None, :]
    kaug = jnp.where(lane < 3, col // ATTN_COL_RADIX * ATTN_COL_RADIX,
                     jnp.where(lane < 6, col % ATTN_COL_RADIX, 0)).astype(BF16)

    def tile_of(pairs, lag):
        def index_map(h, b, t, pq, pk, cnt):
            lists = b * DIFF_HEADS + h
            return b, pairs(pq, pk)[lists * n_pairs + jnp.clip(t - lag, 0, cnt[lists] - 1)], h
        return index_map

    query_tile = lambda pq, pk: pq
    key_tile = lambda pq, pk: pk
    const = lambda *shape: pl.BlockSpec(shape, lambda h, b, t, pq, pk, cnt: (0,) * len(shape))
    pair_buffer = lambda width, dtype: pltpu.VMEM((2, 2 * ts, width), dtype)
    grid_spec = pltpu.PrefetchScalarGridSpec(
        num_scalar_prefetch=3,
        grid=(DIFF_HEADS, B, n_pairs + ATTN_LAG),
        in_specs=[
            pl.BlockSpec(memory_space=pltpu.SMEM),
            pl.BlockSpec((1, ts, LANES), tile_of(query_tile, 0)),
            pl.BlockSpec((1, ts, LANES), tile_of(key_tile, 0)),
            pl.BlockSpec((1, ts, LANES), tile_of(key_tile, ATTN_LAG)),
            const(ts, LANES),
            const(1, DIFF_V_DIM),
        ],
        out_specs=pl.BlockSpec((1, ts, DIFF_V_DIM), tile_of(query_tile, ATTN_LAG)),
        scratch_shapes=[
            pltpu.VMEM((ATTN_SIDES, 2 * ts, 2 * LANES), BF16),
            pltpu.VMEM((ts, 2 * LANES), BF16),
            pltpu.VMEM((ts, 2 * LANES), BF16),
            pltpu.VMEM((2 * ts, ts), F32),
            pair_buffer(ts, F32),
            pair_buffer(LANES, F32),
            pltpu.VMEM((2 * ts, LANES), F32),
            pltpu.VMEM((2 * ts, 2 * DIFF_V_DIM), F32),
        ],
    )
    return pl.pallas_call(
        _diff_attn_kernel,
        grid_spec=grid_spec,
        out_shape=jax.ShapeDtypeStruct((B, S, DIFF_WIDTH), BF16),
        compiler_params=_params("arbitrary", "arbitrary", "arbitrary"),
        name="diff_attn",
    )(pair_q, pair_k, count, scal, dq, dk, dv, kaug, norm_g.reshape(1, DIFF_V_DIM))


def _out_proj_kernel(*refs, has_router):
    (x_ref, pool_ref, hgrn_ref, diff_ref, w_ref, g1_ref, ng_ref, sc_ref, sh_ref) = refs[:9]
    if has_router:
        rw_ref, rb_ref, xo_ref, h2_ref, comb_ref = refs[9:]
    else:
        xo_ref, h2_ref = refs[9:]
    a, b = POOL_WIDTH, POOL_WIDTH + HGRN_WIDTH
    mixed = (_dot(pool_ref[0], w_ref[0:a, :]) + _dot(hgrn_ref[0], w_ref[a:b, :])
             + _dot(diff_ref[0], w_ref[b:, :]))
    x = x_ref[0] + g1_ref[0] * mixed
    xo_ref[0] = x
    h2 = _rms_modulate(x, ng_ref[...], sc_ref[0], sh_ref[0])
    h2_ref[0] = h2.astype(BF16)
    if has_router:
        hh, hl = _split2(h2)
        wh, wl = _split2(rw_ref[...])
        logits = _dot_nt(wh, hh) + _dot_nt(wh, hl) + _dot_nt(wl, hh) + rb_ref[...]
        e_idx = lax.broadcasted_iota(jnp.int32, logits.shape, 0)
        m1 = jnp.max(logits, axis=0, keepdims=True)
        i1 = jnp.min(jnp.where(logits == m1, e_idx, N_EXPERTS), axis=0, keepdims=True)
        first = e_idx == i1
        rest = jnp.where(first, -jnp.inf, logits)
        m2 = jnp.max(rest, axis=0, keepdims=True)
        i2 = jnp.min(jnp.where(rest == m2, e_idx, N_EXPERTS), axis=0, keepdims=True)
        second = e_idx == i2
        e2 = jnp.exp(m2 - m1)
        p1 = 1.0 / (1.0 + e2)
        comb_ref[...] = jnp.where(first, p1, jnp.where(second, e2 * p1, 0.0))


def _out_proj(x, pool_o, hgrn_o, diff_o, w_out, g1, norm_g, sc, sh, router=None, tm=512):
    B, S, D = x.shape
    nt = S // tm
    has_router = router is not None
    row = lambda w: pl.BlockSpec((1, tm, w), lambda b, i: (b, i, 0))
    per_b = pl.BlockSpec((1, 1, D), lambda b, i: (b, 0, 0))
    const = lambda shape: pl.BlockSpec(shape, lambda b, i: (0,) * len(shape))
    in_specs = [row(D), row(POOL_WIDTH), row(HGRN_WIDTH), row(DIFF_WIDTH), const((D, D)),
                per_b, const((1, D)), per_b, per_b]
    args = [x, pool_o, hgrn_o, diff_o, w_out, g1, norm_g.reshape(1, D), sc, sh]
    out_specs = [row(D), row(D)]
    out_shape = [jax.ShapeDtypeStruct((B, S, D), F32), jax.ShapeDtypeStruct((B, S, D), BF16)]
    if has_router:
        rw_t, rb = router
        in_specs += [const((N_EXPERTS, D)), const((N_EXPERTS, 1))]
        args += [rw_t, rb]
        out_specs.append(pl.BlockSpec((N_EXPERTS, tm), lambda b, i: (0, b * nt + i)))
        out_shape.append(jax.ShapeDtypeStruct((N_EXPERTS, B * S), F32))
    return pl.pallas_call(
        functools.partial(_out_proj_kernel, has_router=has_router),
        grid=(B, nt),
        in_specs=in_specs, out_specs=out_specs, out_shape=out_shape,
        compiler_params=_params("parallel", "parallel"),
        name="out_proj",
    )(*args)


def _ffn_kernel(h_ref, wg_ref, wu_ref, wd_ref, o_ref):
    @pl.when(pl.program_id(1) == 0)
    def _():
        o_ref[...] = jnp.zeros_like(o_ref)

    h = h_ref[...]
    mid = _silu(_dot(h, wg_ref[...])) * _dot(h, wu_ref[...])
    o_ref[...] += _dot(mid.astype(BF16), wd_ref[...])


def _ffn(h2, wg, wu, wd, tm=1024, tf=256):
    T, D = h2.shape
    F = wg.shape[-1]
    assert T % tm == 0 and F % tf == 0
    return pl.pallas_call(
        _ffn_kernel,
        grid=(T // tm, F // tf),
        in_specs=[
            pl.BlockSpec((tm, D), lambda i, f: (i, 0)),
            pl.BlockSpec((D, tf), lambda i, f: (0, f)),
            pl.BlockSpec((D, tf), lambda i, f: (0, f)),
            pl.BlockSpec((tf, D), lambda i, f: (f, 0)),
        ],
        out_specs=pl.BlockSpec((tm, D), lambda i, f: (i, 0)),
        out_shape=jax.ShapeDtypeStruct((T, D), F32),
        compiler_params=_params("parallel", "arbitrary"),
        name="ffn",
    )(h2, wg, wu, wd)


MOE_SUB = 256
MOE_TAIL = 128
MOE_RANK_BLOCK = 512


def _moe_kernel(h_ref, comb_ref, wg_ref, wu_ref, wd_ref, o_ref, rank_sc, cnt_sc, xc_sc, yc_sc):
    e = pl.program_id(1)
    fi = pl.program_id(2)
    nf = pl.num_programs(2)
    tm = h_ref.shape[0]

    @pl.when((e == 0) & (fi == 0))
    def _():
        o_ref[...] = jnp.zeros_like(o_ref)
        s_i = lax.broadcasted_iota(jnp.int32, (MOE_RANK_BLOCK, MOE_RANK_BLOCK), 0)
        t_i = lax.broadcasted_iota(jnp.int32, (MOE_RANK_BLOCK, MOE_RANK_BLOCK), 1)
        before = (s_i < t_i).astype(BF16)
        base = jnp.zeros((N_EXPERTS, 1), F32)
        for c0 in range(0, tm, MOE_RANK_BLOCK):
            ind = (comb_ref[:, c0:c0 + MOE_RANK_BLOCK] > 0.0).astype(BF16)
            rank_sc[:, c0:c0 + MOE_RANK_BLOCK] = _dot(ind, before) + base
            base = base + jnp.sum(ind.astype(F32), axis=1, keepdims=True)
        for ex in range(N_EXPERTS):
            cnt_sc[ex] = jnp.sum(base[ex:ex + 1, :]).astype(jnp.int32)

    n_rows = cnt_sc[e]
    n_full = n_rows // MOE_SUB
    rest = n_rows - n_full * MOE_SUB
    n_sub = n_full + (rest > MOE_TAIL).astype(jnp.int32)
    has_tail = (rest > 0) & (rest <= MOE_TAIL)
    sel = comb_ref[pl.ds(e, 1), :]
    rank = rank_sc[pl.ds(e, 1), :]

    def for_blocks(fn):
        def body(sub, carry):
            fn(pl.multiple_of(sub * MOE_SUB, MOE_SUB), MOE_SUB)
            return carry
        lax.fori_loop(0, n_sub, body, 0)

        @pl.when(has_tail)
        def _():
            fn(pl.multiple_of(n_full * MOE_SUB, MOE_SUB), MOE_TAIL)

    def onehot(r0, size):
        slot = lax.broadcasted_iota(jnp.int32, (size, tm), 0) + r0
        return (rank == slot.astype(F32)) & (sel > 0.0)

    @pl.when(fi == 0)
    def _():
        def gather(r0, size):
            g = onehot(r0, size).astype(BF16)
            xc_sc[pl.ds(r0, size), :] = _dot(g, h_ref[...]).astype(BF16)
            yc_sc[pl.ds(r0, size), :] = jnp.zeros((size, yc_sc.shape[1]), F32)
        for_blocks(gather)

    def expert(r0, size):
        xs = xc_sc[pl.ds(r0, size), :]
        mid = _silu(_dot(xs, wg_ref[0])) * _dot(xs, wu_ref[0])
        yc_sc[pl.ds(r0, size), :] += _dot(mid.astype(BF16), wd_ref[0])
    for_blocks(expert)

    @pl.when(fi == nf - 1)
    def _():
        def scatter(r0, size):
            hit = onehot(r0, size)
            weight = jnp.sum(jnp.where(hit, sel, 0.0), axis=1, keepdims=True)
            ys = (yc_sc[pl.ds(r0, size), :] * weight).astype(BF16)
            o_ref[...] = (o_ref[...].astype(F32) + _dot_tn(hit.astype(BF16), ys)).astype(o_ref.dtype)
        for_blocks(scatter)


def _moe(h2, comb, wg, wu, wd, tm=2048, tf=1408):
    T, D = h2.shape
    F = wg.shape[-1]
    assert T % tm == 0 and F % tf == 0
    return pl.pallas_call(
        _moe_kernel,
        grid=(T // tm, N_EXPERTS, F // tf),
        in_specs=[
            pl.BlockSpec((tm, D), lambda i, e, f: (i, 0)),
            pl.BlockSpec((N_EXPERTS, tm), lambda i, e, f: (0, i)),
            pl.BlockSpec((1, D, tf), lambda i, e, f: (e, 0, f)),
            pl.BlockSpec((1, D, tf), lambda i, e, f: (e, 0, f)),
            pl.BlockSpec((1, tf, D), lambda i, e, f: (e, f, 0)),
        ],
        out_specs=pl.BlockSpec((tm, D), lambda i, e, f: (i, 0)),
        out_shape=jax.ShapeDtypeStruct((T, D), BF16),
        scratch_shapes=[
            pltpu.VMEM((N_EXPERTS, tm), F32),
            pltpu.SMEM((N_EXPERTS,), jnp.int32),
            pltpu.VMEM((tm, D), BF16),
            pltpu.VMEM((tm, D), F32),
        ],
        compiler_params=_params("parallel", "arbitrary", "arbitrary"),
        name="moe",
    )(h2, comb, wg, wu, wd)


def _final_kernel(x_ref, f_ref, g2_ref, ng_ref, o_ref):
    x = x_ref[0] + g2_ref[0] * f_ref[0].astype(F32)
    o_ref[0] = x * lax.rsqrt(jnp.mean(x * x, axis=-1, keepdims=True) + RMS_EPS) * ng_ref[...]


def _final(x, f, g2, norm_g, tm=1024):
    B, S, D = x.shape
    row = pl.BlockSpec((1, tm, D), lambda b, i: (b, i, 0))
    return pl.pallas_call(
        _final_kernel,
        grid=(B, S // tm),
        in_specs=[row, row, pl.BlockSpec((1, 1, D), lambda b, i: (b, 0, 0)),
                  pl.BlockSpec((1, D), lambda b, i: (0, 0))],
        out_specs=row,
        out_shape=jax.ShapeDtypeStruct((B, S, D), F32),
        compiler_params=_params("parallel", "parallel"),
        name="final_norm",
    )(x, f, g2, norm_g.reshape(1, D))


def _block_diag(blocks):
    n, r, c = blocks.shape
    out = jnp.zeros((n * r, n * c), blocks.dtype)
    for i in range(n):
        out = out.at[i * r:(i + 1) * r, i * c:(i + 1) * c].set(blocks[i])
    return out


def _trunk(x, ada, prm):
    B, S, D = x.shape
    f = None
    g2 = None
    for l in range(DEPTH):
        sh1, sc1, g1, sh2, sc2, g2_l = [ada[l][:, None, k * D:(k + 1) * D] for k in range(6)]
        x, (u, hg, dq, dk, dv, sq_norms) = _norm_proj(x, f, g2, prm["norm1_g"][l], sc1, sh1, prm["w_in"][l])
        q_norm, k_norm = _tile_norms(sq_norms, S, ATTN_TILE)
        pool_o = _pool(u, prm["pool_bd"][l], prm["pool_scale"][l])
        hgrn_o = _hgrn(hg, prm["lbs"][l, 0], prm["lbs"][l, 1], prm["hgrn_norm_g"][l],
                       prm["ones_bd"], prm["mask_bd"])
        diff_o = _diff_attn(dq, dk, dv, prm["diff_scal"][l], prm["diff_norm_g"][l], q_norm, k_norm)
        moe = l % 2 == 1
        j = l // 2
        router = (prm["router_wt"][j], prm["router_b"][j]) if moe else None
        res = _out_proj(x, pool_o, hgrn_o, diff_o, prm["w_out"][l], g1, prm["norm2_g"][l], sc2, sh2,
                        router=router)
        x = res[0]
        h2 = res[1].reshape(B * S, D)
        if moe:
            f = _moe(h2, res[2], prm["moe_wg"][j], prm["moe_wu"][j], prm["moe_wd"][j])
        else:
            f = _ffn(h2, prm["ffn_wg"][j], prm["ffn_wu"][j], prm["ffn_wd"][j])
        f = f.reshape(B, S, D)
        g2 = g2_l
    return _final(x, f, g2, prm["final_norm_g"])


def kernel(x_prompt, x_sample, c_prompt, c_sample, ada_w, ada_b, norm1_g, norm2_g, w_in, pool_w, pool_scale, hgrn_lb, hgrn_norm_g, diff_lambda, diff_norm_g, w_out, ffn_w_gate, ffn_w_up, ffn_w_down, router_w, router_b, moe_w_gate, moe_w_up, moe_w_down, final_norm_g):
    nb_p, nb_s = c_prompt.shape[0], c_sample.shape[0]
    rows = -(-(nb_p + nb_s) // 8) * 8
    c_all = jnp.zeros((rows, D_MODEL), F32).at[:nb_p].set(c_prompt).at[nb_p:nb_p + nb_s].set(c_sample)
    ada = _ada(c_all, ada_w, ada_b)

    p_lb = jax.nn.softmax(hgrn_lb.astype(F32), axis=0)
    lbs = jnp.cumsum(p_lb, axis=0) - p_lb[0:1]
    lv = diff_lambda.astype(F32)
    lam_init = jnp.asarray([0.8 - 0.6 * math.exp(-0.3 * l) for l in range(DEPTH)], F32)
    lam = jnp.exp(jnp.sum(lv[:, 0] * lv[:, 1], axis=-1)) - jnp.exp(jnp.sum(lv[:, 2] * lv[:, 3], axis=-1)) + lam_init
    slopes = jnp.asarray(ALIBI_SLOPES, F32)
    diff_scal = jnp.concatenate(
        [lam[:, None], (1.0 - lam_init)[:, None], jnp.broadcast_to(slopes, (DEPTH, DIFF_HEADS)),
         jnp.zeros((DEPTH, 2), F32)], axis=1)
    head_blocks = jnp.ones((HGRN_WIDTH // HGRN_HEAD_DIM, HGRN_HEAD_DIM, HGRN_HEAD_DIM), F32)
    prm = {
        "norm1_g": norm1_g, "norm2_g": norm2_g, "final_norm_g": final_norm_g,
        "w_in": w_in.astype(BF16), "w_out": w_out.astype(BF16),
        "pool_bd": jnp.stack([_block_diag(pool_w[l]) for l in range(DEPTH)]).astype(BF16),
        "pool_scale": pool_scale,
        "lbs": lbs,
        "hgrn_norm_g": jnp.tile(hgrn_norm_g, (1, HGRN_WIDTH // HGRN_HEAD_DIM)),
        "ones_bd": _block_diag(head_blocks).astype(BF16),
        "mask_bd": _block_diag(head_blocks),
        "diff_scal": diff_scal, "diff_norm_g": diff_norm_g,
        "ffn_wg": ffn_w_gate.astype(BF16), "ffn_wu": ffn_w_up.astype(BF16), "ffn_wd": ffn_w_down.astype(BF16),
        "router_wt": jnp.swapaxes(router_w, 1, 2), "router_b": router_b[:, :, None],
        "moe_wg": moe_w_gate.astype(BF16), "moe_wu": moe_w_up.astype(BF16), "moe_wd": moe_w_down.astype(BF16),
    }
    y_prompt = _trunk(x_prompt, ada[:, :nb_p], prm)
    y_sample = _trunk(x_sample, ada[:, nb_p:nb_p + nb_s], prm)
    return (y_prompt, y_sample)
```

```python
import functools
import math

import jax
import jax.numpy as jnp
from jax import lax
from jax.experimental import pallas as pl
from jax.experimental.pallas import tpu as pltpu

D_MODEL = 1024
DEPTH = 4
POOL_GROUP_DIM = 64
POOL_WIDTH = 256
POOL_WINDOWS = (2, 4, 8, 16)
POOL_HALO = 16
HGRN_HEAD_DIM = 64
HGRN_WIDTH = 256
DIFF_HEADS = 4
DIFF_HEAD_DIM = 64
DIFF_V_DIM = 128
DIFF_QK_WIDTH = 512
DIFF_WIDTH = 512
D_FF = 2816
N_EXPERTS = 8
RMS_EPS = 1e-6

LANES = 128
VMEM_LIMIT_BYTES = 56 * 1024 * 1024

F32 = jnp.float32
BF16 = jnp.bfloat16


def _params(*semantics):
    return pltpu.CompilerParams(dimension_semantics=semantics, vmem_limit_bytes=VMEM_LIMIT_BYTES)


def _split2(a):
    hi = a.astype(BF16)
    lo = (a - hi.astype(F32)).astype(BF16)
    return hi, lo


def _split3(a):
    hi = a.astype(BF16)
    r = a - hi.astype(F32)
    mid = r.astype(BF16)
    lo = (r - mid.astype(F32)).astype(BF16)
    return hi, mid, lo


def _dot(a, b):
    return jnp.dot(a, b, preferred_element_type=F32)


def _dot_nt(a, b):
    return lax.dot_general(a, b, (((1,), (1,)), ((), ())), preferred_element_type=F32)


def _dot_tn(a, b):
    return lax.dot_general(a, b, (((0,), (0,)), ((), ())), preferred_element_type=F32)


def _dot_f32(a, b):
    ah, al = _split2(a)
    bh, bl = _split2(b)
    return _dot(ah, bh) + _dot(ah, bl) + _dot(al, bh)


def _silu(a):
    return a * jax.nn.sigmoid(a)


def _rms_modulate(x, norm_g, scale, shift):
    y = x * lax.rsqrt(jnp.mean(x * x, axis=-1, keepdims=True) + RMS_EPS) * norm_g
    return y * (1.0 + scale) + shift


def _ada_kernel(c_ref, w_ref, b_ref, o_ref):
    o_ref[0] = _dot_f32(_silu(c_ref[...]), w_ref[0]) + b_ref[0]


def _ada(c_all, ada_w, ada_b):
    rows, d = c_all.shape
    tn = 1024
    n_out = ada_w.shape[-1]
    return pl.pallas_call(
        _ada_kernel,
        grid=(DEPTH, n_out // tn),
        in_specs=[
            pl.BlockSpec((rows, d), lambda l, n: (0, 0)),
            pl.BlockSpec((1, d, tn), lambda l, n: (l, 0, n)),
            pl.BlockSpec((1, 1, tn), lambda l, n: (l, 0, n)),
        ],
        out_specs=pl.BlockSpec((1, rows, tn), lambda l, n: (l, 0, n)),
        out_shape=jax.ShapeDtypeStruct((DEPTH, rows, n_out), F32),
        compiler_params=_params("arbitrary", "arbitrary"),
        name="ada",
    )(c_all, ada_w, ada_b.reshape(DEPTH, 1, n_out))


_PROJ_SPLITS = (
    (0, POOL_WIDTH),
    (POOL_WIDTH, POOL_WIDTH + 5 * HGRN_WIDTH),
    (1536, 2048),
    (2048, 2560),
    (2560, 3072),
)
_PROJ_QK = (2, 3)


def _norm_proj_kernel(*refs, has_f):
    if has_f:
        x_ref, f_ref, g2_ref, ng_ref, sc_ref, sh_ref, w_ref, seg_ref, xo_ref = refs[:9]
        outs = refs[9:]
        x = x_ref[0] + g2_ref[0] * f_ref[0].astype(F32)
        xo_ref[0] = x
    else:
        x_ref, ng_ref, sc_ref, sh_ref, w_ref, seg_ref = refs[:6]
        outs = refs[6:]
        x = x_ref[0]
    hb = _rms_modulate(x, ng_ref[...], sc_ref[0], sh_ref[0]).astype(BF16)
    norms = []
    for k, (o_ref, (a, b)) in enumerate(zip(outs, _PROJ_SPLITS)):
        out = _dot(hb, w_ref[:, a:b]).astype(BF16)
        o_ref[0] = out
        if k in _PROJ_QK:
            sq = out.astype(F32)
            per_row = _dot((sq * sq).astype(BF16), seg_ref[...])
            norms.append(jnp.max(per_row, axis=0, keepdims=True))
    outs[-1][0, 0] = jnp.concatenate(norms, axis=0)


def _norm_proj(x, f, g2, norm_g, sc, sh, w_in, tm=512):
    B, S, D = x.shape
    has_f = f is not None
    row = pl.BlockSpec((1, tm, D), lambda b, i: (b, i, 0))
    per_b = pl.BlockSpec((1, 1, D), lambda b, i: (b, 0, 0))
    in_specs = [row]
    args = [x]
    if has_f:
        in_specs += [row, per_b]
        args += [f, g2]
    seg = (jnp.arange(DIFF_QK_WIDTH)[:, None] // DIFF_HEAD_DIM == jnp.arange(LANES)[None, :]).astype(BF16)
    in_specs += [pl.BlockSpec((1, D), lambda b, i: (0, 0)), per_b, per_b,
                 pl.BlockSpec(w_in.shape, lambda b, i: (0, 0)),
                 pl.BlockSpec(seg.shape, lambda b, i: (0, 0))]
    args += [norm_g.reshape(1, D), sc, sh, w_in, seg]
    out_specs, out_shape = [], []
    if has_f:
        out_specs.append(row)
        out_shape.append(jax.ShapeDtypeStruct((B, S, D), F32))
    for a, b_ in _PROJ_SPLITS:
        out_specs.append(pl.BlockSpec((1, tm, b_ - a), lambda b, i: (b, i, 0)))
        out_shape.append(jax.ShapeDtypeStruct((B, S, b_ - a), BF16))
    out_specs.append(pl.BlockSpec((1, 1, len(_PROJ_QK), LANES), lambda b, i: (b, i, 0, 0)))
    out_shape.append(jax.ShapeDtypeStruct((B, S // tm, len(_PROJ_QK), LANES), F32))
    res = pl.pallas_call(
        functools.partial(_norm_proj_kernel, has_f=has_f),
        grid=(B, S // tm),
        in_specs=in_specs, out_specs=out_specs, out_shape=out_shape,
        compiler_params=_params("parallel", "parallel"),
        name="norm_proj",
    )(*args)
    if has_f:
        return res[0], res[1:]
    return x, res


def _tile_norms(sq_norms, seq_len, ts):
    B, n = sq_norms.shape[:2]
    per_head = sq_norms[..., :2 * DIFF_HEADS].reshape(B, n, len(_PROJ_QK), DIFF_HEADS, 2).max(axis=-1)
    ts = min(ts, seq_len)
    per_tile = per_head.reshape(B, seq_len // ts, n // (seq_len // ts), len(_PROJ_QK), DIFF_HEADS).max(axis=2)
    bound = jnp.sqrt(per_tile).transpose(2, 0, 3, 1)
    return bound[0], bound[1]


def _pool_kernel(prev_ref, u_ref, next_ref, w_ref, scale_ref, o_ref, *, seq_len):
    i = pl.program_id(1)
    n = pl.num_programs(1)
    tm = u_ref.shape[1]
    u = u_ref[0].astype(F32)
    prev = jnp.where(i > 0, prev_ref[0].astype(F32), 0.0)
    nxt = jnp.where(i < n - 1, next_ref[0].astype(F32), 0.0)
    ext = jnp.concatenate([prev, u, nxt], axis=0)
    rows = ext.shape[0]
    group = lax.broadcasted_iota(jnp.int32, (tm, POOL_WIDTH), 1) // POOL_GROUP_DIM
    s = ext
    win = jnp.zeros((tm, POOL_WIDTH), F32)
    for gi, w in enumerate(POOL_WINDOWS):
        s = s + pltpu.roll(s, w // 2, axis=0)
        lead = w // 2 - 1
        centred = s if lead == 0 else pltpu.roll(s, rows - lead, axis=0)
        win = jnp.where(group == gi, centred[POOL_HALO:POOL_HALO + tm], win)
    t = i * tm + lax.broadcasted_iota(jnp.int32, (tm, POOL_WIDTH), 0)
    half = jnp.left_shift(1, group)
    count = jnp.minimum(t + half, seq_len) - jnp.maximum(t - half, 0)
    d = win / count.astype(F32) - u
    o_ref[0] = (_dot(d.astype(BF16), w_ref[...]) * scale_ref[...]).astype(BF16)


def _pool(u, w_bd, scale, tm=512):
    B, S, W = u.shape
    hb = tm // POOL_HALO
    nh = S // POOL_HALO
    return pl.pallas_call(
        functools.partial(_pool_kernel, seq_len=S),
        grid=(B, S // tm),
        in_specs=[
            pl.BlockSpec((1, POOL_HALO, W), lambda b, i: (b, jnp.maximum(i * hb - 1, 0), 0)),
            pl.BlockSpec((1, tm, W), lambda b, i: (b, i, 0)),
            pl.BlockSpec((1, POOL_HALO, W), lambda b, i: (b, jnp.minimum((i + 1) * hb, nh - 1), 0)),
            pl.BlockSpec((W, W), lambda b, i: (0, 0)),
            pl.BlockSpec((1, W), lambda b, i: (0, 0)),
        ],
        out_specs=pl.BlockSpec((1, tm, W), lambda b, i: (b, i, 0)),
        out_shape=jax.ShapeDtypeStruct((B, S, W), BF16),
        compiler_params=_params("parallel", "parallel"),
        name="pool",
    )(u, u, u, w_bd, scale.reshape(1, W))


HGRN_SUB = 16
HGRN_UNROLL = 16


def _hgrn_kernel(qf_ref, zf_ref, vf_ref, qb_ref, zb_ref, vb_ref, lbf_ref, lbb_ref, ones_ref, mask_ref,
                 perm_ref, unperm_ref, of_ref, ob_ref, *scratch):
    c = HGRN_SUB
    tt = qf_ref.shape[1]
    n_sub = tt // c
    ones_bd = ones_ref[...]
    half = len(scratch) // 2
    directions = (
        (False, qf_ref, zf_ref, vf_ref, lbf_ref, of_ref) + tuple(scratch[:half]),
        (True, qb_ref, zb_ref, vb_ref, lbb_ref, ob_ref) + tuple(scratch[half:]),
    )

    @pl.when(pl.program_id(1) == 0)
    def _():
        for d in directions:
            d[6][...] = jnp.zeros_like(d[6])

    r_i = lax.broadcasted_iota(jnp.int32, (tt, tt), 0)
    s_i = lax.broadcasted_iota(jnp.int32, (tt, tt), 1)
    same_sub = (r_i % n_sub) == (s_i % n_sub)
    blk = same_sub.astype(BF16)

    for rev, q_ref, z_ref, v_ref, lb_ref, _, _, qe_sc, kt_sc, vb_sc, gam_sc, acc_sc in directions:
        perm = perm_ref[...]
        q = _silu(_dot(perm, q_ref[0]))
        z = _dot(perm, z_ref[0])
        v = _dot(perm, v_ref[0])
        lb = lb_ref[...]
        f = jnp.maximum(lb + (1.0 - lb) * jax.nn.sigmoid(z), 1e-30)
        g = jnp.log(f)
        kk = (1.0 - lb) * jax.nn.sigmoid(-z)

        tri = (same_sub & ((s_i // n_sub >= r_i // n_sub) if rev
                           else (s_i // n_sub <= r_i // n_sub))).astype(BF16)
        parts = _split3(g)
        cum = sum(_dot(tri, p) for p in parts)
        tot = sum(_dot(blk, p) for p in parts)

        unperm = unperm_ref[...]
        qe_sc[...] = _dot(unperm, (q * jnp.exp(cum)).astype(BF16)).astype(BF16)
        kt_sc[...] = _dot(unperm, (kk * jnp.exp(tot - cum)).astype(BF16)).astype(BF16)
        vb_sc[...] = v_ref[0]
        gam_sc[...] = jnp.exp(tot[0:n_sub])

        kd = kk
        acc_sc[...] = _dot((q * kk).astype(BF16), ones_bd) * v
        for delta in range(1, c):
            n = tt - n_sub * delta
            tgt = slice(0, n) if rev else slice(n_sub * delta, tt)
            src = slice(n_sub * delta, tt) if rev else slice(0, n)
            kd = (kd[n_sub:] if rev else kd[:n]) * f[tgt]
            acc_sc[tgt] += _dot((q[tgt] * kd).astype(BF16), ones_bd) * v[src]
        acc_sc[...] = sum(_dot(unperm, p) for p in _split3(acc_sc[...]))

    mask = mask_ref[0:LANES, 0:LANES]

    def body(ci, carry):
        for rev, _, _, _, _, _, st_ref, qe_sc, kt_sc, vb_sc, gam_sc, acc_sc in directions:
            idx = (n_sub - 1 - ci) if rev else ci
            r0 = pl.multiple_of(idx * c, c)
            gam = gam_sc[pl.ds(idx, 1), :]
            for lanes in (slice(0, LANES), slice(LANES, 2 * LANES)):
                st = st_ref[lanes, lanes]
                acc_sc[pl.ds(r0, c), lanes] += _dot_nt(qe_sc[pl.ds(r0, c), lanes], st.astype(BF16))
                kv = _dot_tn(vb_sc[pl.ds(r0, c), lanes], kt_sc[pl.ds(r0, c), lanes])
                st_ref[lanes, lanes] = st * gam[:, lanes] + kv * mask
        return carry

    lax.fori_loop(0, n_sub, body, 0, unroll=HGRN_UNROLL)

    for d in directions:
        d[5][0] = d[11][...]


def _hgrn_finish_kernel(of_ref, ob_ref, gate_ref, ng_ref, ones_ref, o_ref):
    o = of_ref[0] + ob_ref[0]
    ms = sum(_dot(p, ones_ref[...]) for p in _split2(o * o)) * (1.0 / HGRN_HEAD_DIM)
    o = o * lax.rsqrt(ms + RMS_EPS) * ng_ref[...] * _silu(gate_ref[0].astype(F32))
    o_ref[0] = o.astype(BF16)


def _hgrn(hg, lb_fwd, lb_bwd, norm_g, ones_bd, mask_bd, tt=256, tm=1024):
    B, S, _ = hg.shape
    W = HGRN_WIDTH
    nt = S // tt

    def col(k, rev):
        return pl.BlockSpec((1, tt, W), lambda b, j: (b, nt - 1 - j if rev else j, k))

    const = lambda shape: pl.BlockSpec(shape, lambda b, j: (0, 0))
    per_direction = [
        pltpu.VMEM((W, W), F32),
        pltpu.VMEM((tt, W), BF16),
        pltpu.VMEM((tt, W), BF16),
        pltpu.VMEM((tt, W), BF16),
        pltpu.VMEM((tt // HGRN_SUB, W), F32),
        pltpu.VMEM((tt, W), F32),
    ]
    token = jnp.arange(tt)
    perm = (token[None, :] == (token % (tt // HGRN_SUB) * HGRN_SUB + token // (tt // HGRN_SUB))[:, None])
    perm = perm.astype(BF16)
    o_fwd, o_bwd = pl.pallas_call(
        _hgrn_kernel,
        grid=(B, nt),
        in_specs=[col(0, False), col(1, False), col(3, False), col(0, True), col(2, True), col(3, True),
                  const((1, W)), const((1, W)), const((W, W)), const((W, W)),
                  const((tt, tt)), const((tt, tt))],
        out_specs=[col(0, False), col(0, True)],
        out_shape=[jax.ShapeDtypeStruct((B, S, W), F32)] * 2,
        scratch_shapes=per_direction * 2,
        compiler_params=_params("parallel", "arbitrary"),
        name="hgrn",
    )(hg, hg, hg, hg, hg, hg, lb_fwd.reshape(1, W), lb_bwd.reshape(1, W), ones_bd, mask_bd, perm, perm.T)

    tm = min(tm, S)
    row = pl.BlockSpec((1, tm, W), lambda b, i: (b, i, 0))
    return pl.pallas_call(
        _hgrn_finish_kernel,
        grid=(B, S // tm),
        in_specs=[row, row, pl.BlockSpec((1, tm, W), lambda b, i: (b, i, 4)),
                  pl.BlockSpec((1, W), lambda b, i: (0, 0)), pl.BlockSpec((W, W), lambda b, i: (0, 0))],
        out_specs=row,
        out_shape=jax.ShapeDtypeStruct((B, S, W), BF16),
        compiler_params=_params("parallel", "parallel"),
        name="hgrn_finish",
    )(o_fwd, o_bwd, hg, norm_g.reshape(1, W), ones_bd)


LOG2E = 1.4426950408889634
ATTN_COL_RADIX = 32
ATTN_SIDES = 3
ATTN_ROW_CHUNK = 256
ATTN_LAG = 1
ATTN_TILE = 1024


def _diff_attn_kernel(pq_ref, pk_ref, cnt_ref, scal_ref, q_ref, k_ref, v_ref, kaug_ref, ng_ref, o_ref,
                      qx_sc, kf_sc, vf_sc, bias_sc, s_sc, pmax_sc, m_sc, acc_sc):
    h = pl.program_id(0)
    t = pl.program_id(2)
    lists = pl.program_id(1) * pl.num_programs(0) + h
    n_pairs = cnt_ref[lists]
    base = lists * (pl.num_programs(2) - ATTN_LAG)
    tq = q_ref.shape[1]
    tk = k_ref.shape[1]
    sigma = scal_ref[2 + h] * LOG2E

    def pair_at(step):
        i = base + jnp.clip(step, 0, n_pairs - 1)
        return pq_ref[i], pk_ref[i]

    qi, ki = pair_at(t)
    qi_done, ki_done = pair_at(t - 1)
    new_query_tile = (t == 0) | (qi != qi_done)
    done_first_of_tile = (t <= 1) | (qi_done != pair_at(t - 2)[0])
    done_last_of_tile = (t >= 1) & (t <= n_pairs) & ((t == n_pairs) | (qi != qi_done))
    chunks = range(0, 2 * tq, ATTN_ROW_CHUNK)

    def sigma_digits(lane_idx, first_lane):
        hi, mid, lo = [p.astype(F32) for p in _split3(jnp.full(lane_idx.shape, sigma, F32))]
        part = (lane_idx - first_lane) % 3
        digits = jnp.where(part == 0, hi, jnp.where(part == 1, mid, lo))
        return jnp.where((lane_idx >= first_lane) & (lane_idx < first_lane + 6), digits, 0.0)

    @pl.when((pl.program_id(1) == 0) & (t == 0))
    def _():
        vcol = lax.broadcasted_iota(jnp.int32, (tk, LANES), 1)
        kf_sc[:, LANES:2 * LANES] = jnp.where(
            vcol < 6, kaug_ref[...].astype(F32), sigma_digits(vcol, 6)).astype(BF16)
        vf_sc[:, LANES:2 * LANES] = jnp.where(vcol == 0, 1.0, 0.0).astype(BF16)
        for r0 in chunks:
            rr = (r0 + lax.broadcasted_iota(jnp.int32, (ATTN_ROW_CHUNK, tk), 0)) % tq
            cc = lax.broadcasted_iota(jnp.int32, (ATTN_ROW_CHUNK, tk), 1)
            bias_sc[r0:r0 + ATTN_ROW_CHUNK] = -sigma * jnp.abs(rr - cc).astype(F32)
        s_sc[1] = jnp.zeros(s_sc.shape[1:], F32)
        pmax_sc[1] = jnp.zeros(pmax_sc.shape[1:], F32)
        m_sc[...] = jnp.zeros(m_sc.shape, F32)
        acc_sc[...] = jnp.zeros(acc_sc.shape, F32)

    @pl.when(new_query_tile)
    def _():
        q = q_ref[0].astype(F32) * (DIFF_HEAD_DIM ** -0.5 * LOG2E)
        lane = lax.broadcasted_iota(jnp.int32, q.shape, 1)
        first = lane < DIFF_HEAD_DIM
        qa = jnp.where(first, q, 0.0).astype(BF16)
        qb = jnp.where(first, 0.0, q).astype(BF16)
        row = lax.broadcasted_iota(jnp.int32, q.shape, 0)
        row_digits = jnp.where(lane < 9, row // ATTN_COL_RADIX * ATTN_COL_RADIX, row % ATTN_COL_RADIX)
        aug = sigma_digits(lane, 0) - jnp.where((lane >= 6) & (lane < 12), row_digits.astype(F32), 0.0)
        for side, a in enumerate((aug, -aug, jnp.zeros_like(aug))):
            qx_sc[side, 0:tq, 0:LANES] = qa
            qx_sc[side, tq:2 * tq, 0:LANES] = qb
            qx_sc[side, 0:tq, LANES:2 * LANES] = a.astype(BF16)
            qx_sc[side, tq:2 * tq, LANES:2 * LANES] = a.astype(BF16)

    n_col = tk // LANES

    kf_sc[:, 0:LANES] = k_ref[0]
    vf_sc[:, 0:LANES] = v_ref[0]
    side = jnp.where(ki < qi, 0, jnp.where(ki > qi, 1, 2))

    def score(overlapping, cur):
        for r0 in chunks:
            rows = slice(r0, r0 + ATTN_ROW_CHUNK)
            s = _dot_nt(qx_sc[side, rows, :], kf_sc[...])
            if overlapping:
                s = s + bias_sc[rows]
            s_sc[cur, rows] = s
            pmax_sc[cur, rows] = functools.reduce(
                jnp.maximum, [s[:, c * LANES:(c + 1) * LANES] for c in range(n_col)])

    def accumulate(old):
        tile_const = -sigma * (jnp.abs(qi_done - ki_done) * tq).astype(F32)
        for r0 in chunks:
            rows = slice(r0, r0 + ATTN_ROW_CHUNK)
            m_old = jnp.where(done_first_of_tile, -jnp.inf, m_sc[rows])
            row_max = jnp.max(pmax_sc[old, rows], axis=-1, keepdims=True) + tile_const
            m_new = jnp.maximum(m_old, row_max)
            alpha = jnp.exp2(m_old - m_new)
            shift = jnp.concatenate([m_new - tile_const] * n_col, axis=1)
            p = jnp.exp2((s_sc[old, rows] - shift).astype(BF16))
            acc_sc[rows] = jnp.concatenate([alpha, alpha], axis=1) * acc_sc[rows] + _dot(p, vf_sc[...])
            m_sc[rows] = m_new

    for parity in (0, 1):
        for overlapping in (False, True):
            @pl.when((t <= n_pairs) & (t % 2 == parity) & ((side == 2) == overlapping))
            def _():
                accumulate(1 - parity)
                score(overlapping, parity)

    @pl.when(done_last_of_tile)
    def _():
        lam = scal_ref[0]
        out_scale = scal_ref[1]
        acc = acc_sc[...]
        o_all = acc[:, 0:DIFF_V_DIM] / acc[:, DIFF_V_DIM:DIFF_V_DIM + 1]
        o = o_all[0:tq] - lam * o_all[tq:2 * tq]
        o = o * lax.rsqrt(jnp.mean(o * o, axis=-1, keepdims=True) + RMS_EPS)
        o_ref[0] = (o * ng_ref[...] * out_scale).astype(BF16)


ATTN_SKIP_MARGIN = 60.0
ATTN_NORM_SLACK = 1.02
ALIBI_SLOPES = tuple(2.0 ** (-8.0 * (h + 1) / DIFF_HEADS) for h in range(DIFF_HEADS))


def _attn_pair_lists(q_norm, k_norm, ts):
    B, H, n = q_norm.shape
    scale = ATTN_NORM_SLACK * DIFF_HEAD_DIM ** -0.5
    i = jnp.arange(n)
    gap = jnp.abs(i[:, None] - i[None, :])
    closest = jnp.where(gap == 0, 0, (gap - 1) * ts + 1).astype(F32)
    slopes = jnp.asarray(ALIBI_SLOPES, F32)[None, :, None, None]
    best = scale * q_norm[:, :, :, None] * k_norm[:, :, None, :] - slopes * closest[None, None]
    floor = -scale * q_norm * k_norm
    active = (best >= floor[:, :, :, None] - ATTN_SKIP_MARGIN) | (gap == 0)[None, None]
    flat = active.reshape(B, H, n * n)
    idx = jnp.arange(n * n, dtype=jnp.int32)
    order = jnp.argsort(jnp.where(flat, idx, idx + n * n), axis=-1).astype(jnp.int32)
    count = jnp.sum(flat, axis=-1).astype(jnp.int32)
    order = jnp.take_along_axis(order, jnp.minimum(idx[None, None], count[..., None] - 1), axis=-1)
    return (order // n).reshape(-1), (order % n).reshape(-1), count.reshape(-1)


def _diff_attn(dq, dk, dv, scal, norm_g, q_norm, k_norm, ts=ATTN_TILE):
    B, S, _ = dq.shape
    ts = min(ts, S)
    assert S % ts == 0 and (2 * ts) % ATTN_ROW_CHUNK == 0 and ts <= ATTN_COL_RADIX * ATTN_COL_RADIX
    nk = S // ts
    n_pairs = nk * nk
    pair_q, pair_k, count = _attn_pair_lists(q_norm, k_norm, ts)
    col = jnp.arange(ts, dtype=jnp.int32)[:, None]
    lane = jnp.arange(LANES, dtype=jnp.int32)[None, :]
    kaug = jnp.where(lane < 3, col // ATTN_COL_RADIX * ATTN_COL_RADIX,
                     jnp.where(lane < 6, col % ATTN_COL_RADIX, 0)).astype(BF16)

    def tile_of(pairs, lag):
        def index_map(h, b, t, pq, pk, cnt):
            lists = b * DIFF_HEADS + h
            return b, pairs(pq, pk)[lists * n_pairs + jnp.clip(t - lag, 0, cnt[lists] - 1)], h
        return index_map

    query_tile = lambda pq, pk: pq
    key_tile = lambda pq, pk: pk
    const = lambda *shape: pl.BlockSpec(shape, lambda h, b, t, pq, pk, cnt: (0,) * len(shape))
    pair_buffer = lambda width, dtype: pltpu.VMEM((2, 2 * ts, width), dtype)
    grid_spec = pltpu.PrefetchScalarGridSpec(
        num_scalar_prefetch=3,
        grid=(DIFF_HEADS, B, n_pairs + ATTN_LAG),
        in_specs=[
            pl.BlockSpec(memory_space=pltpu.SMEM),
            pl.BlockSpec((1, ts, LANES), tile_of(query_tile, 0)),
            pl.BlockSpec((1, ts, LANES), tile_of(key_tile, 0)),
            pl.BlockSpec((1, ts, LANES), tile_of(key_tile, ATTN_LAG)),
            const(ts, LANES),
            const(1, DIFF_V_DIM),
        ],
        out_specs=pl.BlockSpec((1, ts, DIFF_V_DIM), tile_of(query_tile, ATTN_LAG)),
        scratch_shapes=[
            pltpu.VMEM((ATTN_SIDES, 2 * ts, 2 * LANES), BF16),
            pltpu.VMEM((ts, 2 * LANES), BF16),
            pltpu.VMEM((ts, 2 * LANES), BF16),
            pltpu.VMEM((2 * ts, ts), F32),
            pair_buffer(ts, F32),
            pair_buffer(LANES, F32),
            pltpu.VMEM((2 * ts, LANES), F32),
            pltpu.VMEM((2 * ts, 2 * DIFF_V_DIM), F32),
        ],
    )
    return pl.pallas_call(
        _diff_attn_kernel,
        grid_spec=grid_spec,
        out_shape=jax.ShapeDtypeStruct((B, S, DIFF_WIDTH), BF16),
        compiler_params=_params("arbitrary", "arbitrary", "arbitrary"),
        name="diff_attn",
    )(pair_q, pair_k, count, scal, dq, dk, dv, kaug, norm_g.reshape(1, DIFF_V_DIM))


def _out_proj_kernel(*refs, has_router):
    (x_ref, pool_ref, hgrn_ref, diff_ref, w_ref, g1_ref, ng_ref, sc_ref, sh_ref) = refs[:9]
    if has_router:
        rw_ref, rb_ref, xo_ref, h2_ref, comb_ref = refs[9:]
    else:
        xo_ref, h2_ref = refs[9:]
    a, b = POOL_WIDTH, POOL_WIDTH + HGRN_WIDTH
    mixed = (_dot(pool_ref[0], w_ref[0:a, :]) + _dot(hgrn_ref[0], w_ref[a:b, :])
             + _dot(diff_ref[0], w_ref[b:, :]))
    x = x_ref[0] + g1_ref[0] * mixed
    xo_ref[0] = x
    h2 = _rms_modulate(x, ng_ref[...], sc_ref[0], sh_ref[0])
    h2_ref[0] = h2.astype(BF16)
    if has_router:
        hh, hl = _split2(h2)
        wh, wl = _split2(rw_ref[...])
        logits = _dot_nt(wh, hh) + _dot_nt(wh, hl) + _dot_nt(wl, hh) + rb_ref[...]
        e_idx = lax.broadcasted_iota(jnp.int32, logits.shape, 0)
        m1 = jnp.max(logits, axis=0, keepdims=True)
        i1 = jnp.min(jnp.where(logits == m1, e_idx, N_EXPERTS), axis=0, keepdims=True)
        first = e_idx == i1
        rest = jnp.where(first, -jnp.inf, logits)
        m2 = jnp.max(rest, axis=0, keepdims=True)
        i2 = jnp.min(jnp.where(rest == m2, e_idx, N_EXPERTS), axis=0, keepdims=True)
        second = e_idx == i2
        e2 = jnp.exp(m2 - m1)
        p1 = 1.0 / (1.0 + e2)
        comb_ref[...] = jnp.where(first, p1, jnp.where(second, e2 * p1, 0.0))


def _out_proj(x, pool_o, hgrn_o, diff_o, w_out, g1, norm_g, sc, sh, router=None, tm=512):
    B, S, D = x.shape
    nt = S // tm
    has_router = router is not None
    row = lambda w: pl.BlockSpec((1, tm, w), lambda b, i: (b, i, 0))
    per_b = pl.BlockSpec((1, 1, D), lambda b, i: (b, 0, 0))
    const = lambda shape: pl.BlockSpec(shape, lambda b, i: (0,) * len(shape))
    in_specs = [row(D), row(POOL_WIDTH), row(HGRN_WIDTH), row(DIFF_WIDTH), const((D, D)),
                per_b, const((1, D)), per_b, per_b]
    args = [x, pool_o, hgrn_o, diff_o, w_out, g1, norm_g.reshape(1, D), sc, sh]
    out_specs = [row(D), row(D)]
    out_shape = [jax.ShapeDtypeStruct((B, S, D), F32), jax.ShapeDtypeStruct((B, S, D), BF16)]
    if has_router:
        rw_t, rb = router
        in_specs += [const((N_EXPERTS, D)), const((N_EXPERTS, 1))]
        args += [rw_t, rb]
        out_specs.append(pl.BlockSpec((N_EXPERTS, tm), lambda b, i: (0, b * nt + i)))
        out_shape.append(jax.ShapeDtypeStruct((N_EXPERTS, B * S), F32))
    return pl.pallas_call(
        functools.partial(_out_proj_kernel, has_router=has_router),
        grid=(B, nt),
        in_specs=in_specs, out_specs=out_specs, out_shape=out_shape,
        compiler_params=_params("parallel", "parallel"),
        name="out_proj",
    )(*args)


def _ffn_kernel(h_ref, wg_ref, wu_ref, wd_ref, o_ref):
    @pl.when(pl.program_id(1) == 0)
    def _():
        o_ref[...] = jnp.zeros_like(o_ref)

    h = h_ref[...]
    mid = _silu(_dot(h, wg_ref[...])) * _dot(h, wu_ref[...])
    o_ref[...] += _dot(mid.astype(BF16), wd_ref[...])


def _ffn(h2, wg, wu, wd, tm=1024, tf=256):
    T, D = h2.shape
    F = wg.shape[-1]
    assert T % tm == 0 and F % tf == 0
    return pl.pallas_call(
        _ffn_kernel,
        grid=(T // tm, F // tf),
        in_specs=[
            pl.BlockSpec((tm, D), lambda i, f: (i, 0)),
            pl.BlockSpec((D, tf), lambda i, f: (0, f)),
            pl.BlockSpec((D, tf), lambda i, f: (0, f)),
            pl.BlockSpec((tf, D), lambda i, f: (f, 0)),
        ],
        out_specs=pl.BlockSpec((tm, D), lambda i, f: (i, 0)),
        out_shape=jax.ShapeDtypeStruct((T, D), F32),
        compiler_params=_params("parallel", "arbitrary"),
        name="ffn",
    )(h2, wg, wu, wd)


MOE_SUB = 256
MOE_TAILS = (64, 128)
MOE_RANK_BLOCK = 512


def _moe_kernel(h_ref, comb_ref, wg_ref, wu_ref, wd_ref, o_ref, rank_sc, cnt_sc, xc_sc, yc_sc):
    e = pl.program_id(1)
    fi = pl.program_id(2)
    nf = pl.num_programs(2)
    tm = h_ref.shape[0]

    @pl.when((e == 0) & (fi == 0))
    def _():
        o_ref[...] = jnp.zeros_like(o_ref)
        s_i = lax.broadcasted_iota(jnp.int32, (MOE_RANK_BLOCK, MOE_RANK_BLOCK), 0)
        t_i = lax.broadcasted_iota(jnp.int32, (MOE_RANK_BLOCK, MOE_RANK_BLOCK), 1)
        before = (s_i < t_i).astype(BF16)
        base = jnp.zeros((N_EXPERTS, 1), F32)
        for c0 in range(0, tm, MOE_RANK_BLOCK):
            ind = (comb_ref[:, c0:c0 + MOE_RANK_BLOCK] > 0.0).astype(BF16)
            rank_sc[:, c0:c0 + MOE_RANK_BLOCK] = _dot(ind, before) + base
            base = base + jnp.sum(ind.astype(F32), axis=1, keepdims=True)
        for ex in range(N_EXPERTS):
            cnt_sc[ex] = jnp.sum(base[ex:ex + 1, :]).astype(jnp.int32)

    n_rows = cnt_sc[e]
    n_full = n_rows // MOE_SUB
    rest = n_rows - n_full * MOE_SUB
    n_sub = n_full + (rest > MOE_TAILS[-1]).astype(jnp.int32)
    sel = comb_ref[pl.ds(e, 1), :]
    rank = rank_sc[pl.ds(e, 1), :]

    def for_blocks(fn):
        def body(sub, carry):
            fn(pl.multiple_of(sub * MOE_SUB, MOE_SUB), MOE_SUB)
            return carry
        lax.fori_loop(0, n_sub, body, 0)

        for smaller, size in zip((0,) + MOE_TAILS, MOE_TAILS):
            @pl.when((rest > smaller) & (rest <= size))
            def _():
                fn(pl.multiple_of(n_full * MOE_SUB, MOE_SUB), size)

    def onehot(r0, size):
        slot = lax.broadcasted_iota(jnp.int32, (size, tm), 0) + r0
        return (rank == slot.astype(F32)) & (sel > 0.0)

    @pl.when(fi == 0)
    def _():
        def gather(r0, size):
            g = onehot(r0, size).astype(BF16)
            xc_sc[pl.ds(r0, size), :] = _dot(g, h_ref[...]).astype(BF16)
            yc_sc[pl.ds(r0, size), :] = jnp.zeros((size, yc_sc.shape[1]), F32)
        for_blocks(gather)

    def expert(r0, size):
        xs = xc_sc[pl.ds(r0, size), :]
        mid = _silu(_dot(xs, wg_ref[0])) * _dot(xs, wu_ref[0])
        yc_sc[pl.ds(r0, size), :] += _dot(mid.astype(BF16), wd_ref[0])
    for_blocks(expert)

    @pl.when(fi == nf - 1)
    def _():
        def scatter(r0, size):
            hit = onehot(r0, size)
            weight = jnp.sum(jnp.where(hit, sel, 0.0), axis=1, keepdims=True)
            ys = (yc_sc[pl.ds(r0, size), :] * weight).astype(BF16)
            o_ref[...] = (o_ref[...].astype(F32) + _dot_tn(hit.astype(BF16), ys)).astype(o_ref.dtype)
        for_blocks(scatter)


def _moe(h2, comb, wg, wu, wd, tm=2048, tf=1408):
    T, D = h2.shape
    F = wg.shape[-1]
    assert T % tm == 0 and F % tf == 0
    return pl.pallas_call(
        _moe_kernel,
        grid=(T // tm, N_EXPERTS, F // tf),
        in_specs=[
            pl.BlockSpec((tm, D), lambda i, e, f: (i, 0)),
            pl.BlockSpec((N_EXPERTS, tm), lambda i, e, f: (0, i)),
            pl.BlockSpec((1, D, tf), lambda i, e, f: (e, 0, f)),
            pl.BlockSpec((1, D, tf), lambda i, e, f: (e, 0, f)),
            pl.BlockSpec((1, tf, D), lambda i, e, f: (e, f, 0)),
        ],
        out_specs=pl.BlockSpec((tm, D), lambda i, e, f: (i, 0)),
        out_shape=jax.ShapeDtypeStruct((T, D), BF16),
        scratch_shapes=[
            pltpu.VMEM((N_EXPERTS, tm), F32),
            pltpu.SMEM((N_EXPERTS,), jnp.int32),
            pltpu.VMEM((tm, D), BF16),
            pltpu.VMEM((tm, D), F32),
        ],
        compiler_params=_params("parallel", "arbitrary", "arbitrary"),
        name="moe",
    )(h2, comb, wg, wu, wd)


def _final_kernel(x_ref, f_ref, g2_ref, ng_ref, o_ref):
    x = x_ref[0] + g2_ref[0] * f_ref[0].astype(F32)
    o_ref[0] = x * lax.rsqrt(jnp.mean(x * x, axis=-1, keepdims=True) + RMS_EPS) * ng_ref[...]


def _final(x, f, g2, norm_g, tm=1024):
    B, S, D = x.shape
    row = pl.BlockSpec((1, tm, D), lambda b, i: (b, i, 0))
    return pl.pallas_call(
        _final_kernel,
        grid=(B, S // tm),
        in_specs=[row, row, pl.BlockSpec((1, 1, D), lambda b, i: (b, 0, 0)),
                  pl.BlockSpec((1, D), lambda b, i: (0, 0))],
        out_specs=row,
        out_shape=jax.ShapeDtypeStruct((B, S, D), F32),
        compiler_params=_params("parallel", "parallel"),
        name="final_norm",
    )(x, f, g2, norm_g.reshape(1, D))


def _block_diag(blocks):
    n, r, c = blocks.shape
    out = jnp.zeros((n * r, n * c), blocks.dtype)
    for i in range(n):
        out = out.at[i * r:(i + 1) * r, i * c:(i + 1) * c].set(blocks[i])
    return out


def _trunk(x, ada, prm):
    B, S, D = x.shape
    f = None
    g2 = None
    for l in range(DEPTH):
        sh1, sc1, g1, sh2, sc2, g2_l = [ada[l][:, None, k * D:(k + 1) * D] for k in range(6)]
        x, (u, hg, dq, dk, dv, sq_norms) = _norm_proj(x, f, g2, prm["norm1_g"][l], sc1, sh1, prm["w_in"][l])
        q_norm, k_norm = _tile_norms(sq_norms, S, ATTN_TILE)
        pool_o = _pool(u, prm["pool_bd"][l], prm["pool_scale"][l])
        hgrn_o = _hgrn(hg, prm["lbs"][l, 0], prm["lbs"][l, 1], prm["hgrn_norm_g"][l],
                       prm["ones_bd"], prm["mask_bd"])
        diff_o = _diff_attn(dq, dk, dv, prm["diff_scal"][l], prm["diff_norm_g"][l], q_norm, k_norm)
        moe = l % 2 == 1
        j = l // 2
        router = (prm["router_wt"][j], prm["router_b"][j]) if moe else None
        res = _out_proj(x, pool_o, hgrn_o, diff_o, prm["w_out"][l], g1, prm["norm2_g"][l], sc2, sh2,
                        router=router)
        x = res[0]
        h2 = res[1].reshape(B * S, D)
        if moe:
            f = _moe(h2, res[2], prm["moe_wg"][j], prm["moe_wu"][j], prm["moe_wd"][j])
        else:
            f = _ffn(h2, prm["ffn_wg"][j], prm["ffn_wu"][j], prm["ffn_wd"][j])
        f = f.reshape(B, S, D)
        g2 = g2_l
    return _final(x, f, g2, prm["final_norm_g"])


def kernel(x_prompt, x_sample, c_prompt, c_sample, ada_w, ada_b, norm1_g, norm2_g, w_in, pool_w, pool_scale, hgrn_lb, hgrn_norm_g, diff_lambda, diff_norm_g, w_out, ffn_w_gate, ffn_w_up, ffn_w_down, router_w, router_b, moe_w_gate, moe_w_up, moe_w_down, final_norm_g):
    nb_p, nb_s = c_prompt.shape[0], c_sample.shape[0]
    rows = -(-(nb_p + nb_s) // 8) * 8
    c_all = jnp.zeros((rows, D_MODEL), F32).at[:nb_p].set(c_prompt).at[nb_p:nb_p + nb_s].set(c_sample)
    ada = _ada(c_all, ada_w, ada_b)

    p_lb = jax.nn.softmax(hgrn_lb.astype(F32), axis=0)
    lbs = jnp.cumsum(p_lb, axis=0) - p_lb[0:1]
    lv = diff_lambda.astype(F32)
    lam_init = jnp.asarray([0.8 - 0.6 * math.exp(-0.3 * l) for l in range(DEPTH)], F32)
    lam = jnp.exp(jnp.sum(lv[:, 0] * lv[:, 1], axis=-1)) - jnp.exp(jnp.sum(lv[:, 2] * lv[:, 3], axis=-1)) + lam_init
    slopes = jnp.asarray(ALIBI_SLOPES, F32)
    diff_scal = jnp.concatenate(
        [lam[:, None], (1.0 - lam_init)[:, None], jnp.broadcast_to(slopes, (DEPTH, DIFF_HEADS)),
         jnp.zeros((DEPTH, 2), F32)], axis=1)
    head_blocks = jnp.ones((HGRN_WIDTH // HGRN_HEAD_DIM, HGRN_HEAD_DIM, HGRN_HEAD_DIM), F32)
    prm = {
        "norm1_g": norm1_g, "norm2_g": norm2_g, "final_norm_g": final_norm_g,
        "w_in": w_in.astype(BF16), "w_out": w_out.astype(BF16),
        "pool_bd": jnp.stack([_block_diag(pool_w[l]) for l in range(DEPTH)]).astype(BF16),
        "pool_scale": pool_scale,
        "lbs": lbs,
        "hgrn_norm_g": jnp.tile(hgrn_norm_g, (1, HGRN_WIDTH // HGRN_HEAD_DIM)),
        "ones_bd": _block_diag(head_blocks).astype(BF16),
        "mask_bd": _block_diag(head_blocks),
        "diff_scal": diff_scal, "diff_norm_g": diff_norm_g,
        "ffn_wg": ffn_w_gate.astype(BF16), "ffn_wu": ffn_w_up.astype(BF16), "ffn_wd": ffn_w_down.astype(BF16),
        "router_wt": jnp.swapaxes(router_w, 1, 2), "router_b": router_b[:, :, None],
        "moe_wg": moe_w_gate.astype(BF16), "moe_wu": moe_w_up.astype(BF16), "moe_wd": moe_w_down.astype(BF16),
    }
    y_prompt = _trunk(x_prompt, ada[:, :nb_p], prm)
    y_sample = _trunk(x_sample, ada[:, nb_p:nb_p + nb_s], prm)
    return (y_prompt, y_sample)
```
